```python
import math
import jax
import jax.numpy as jnp
from jax import lax
import numpy as np

D_MODEL = 1024
BATCH = 2
SEQ = 8192
DEPTH = 1
DEC_BATCH = 1
DEC_SEQ = 16384
PAST_LEN = 128

MLA_HEADS = 4
QK_NOPE = 128
QK_ROPE = 64
V_HEAD = 128
Q_LORA = 384
KV_LORA = 256
MLA_WIDTH = MLA_HEADS * V_HEAD
ROPE_THETA = 10000.0
Q_BLOCK = 128
SSM_GROUPS = 32
SSM_GROUP_CH = 16
SSM_WIDTH = SSM_GROUPS * SSM_GROUP_CH
SSM_STATE = 64
DT_MIN = 0.001
DT_MAX = 0.1
MIX_WIDTH = MLA_WIDTH + SSM_WIDTH
IN_WIDTH = Q_LORA + KV_LORA + QK_ROPE + SSM_WIDTH
PEER_HEADS = 8
PEER_NKEYS = 128
PEER_EXPERTS = PEER_NKEYS * PEER_NKEYS
PEER_DKEY = 256
PEER_TOPK = 16
TOKEN_BLOCK = 128
EPS = 1e-6

kernel_name = 'hybrid_mla_s5_peer_encoder'


def rmsnorm(x, g):
    xf = x.astype(jnp.float32)
    y = xf * lax.rsqrt(jnp.mean(xf * xf, axis=-1, keepdims=True) + EPS)
    return (y * g.astype(jnp.float32)).astype(x.dtype)


def rope_tables(seq):
    inv = ROPE_THETA ** (-jnp.arange(0, QK_ROPE, 2, dtype=jnp.float32) / QK_ROPE)
    ang = jnp.arange(seq, dtype=jnp.float32)[:, None] * inv[None, :]
    return jnp.cos(ang), jnp.sin(ang)


def apply_rope(x, cos, sin):
    x1, x2 = jnp.split(x, 2, axis=-1)
    c = cos.astype(x.dtype)
    s = sin.astype(x.dtype)
    return jnp.concatenate([x1 * c - x2 * s, x1 * s + x2 * c], axis=-1)


def mla_group(c_q, c_kv, k_pe, q_norm, kv_norm, w_uq, w_ukv):
    bsz, s, _ = c_q.shape
    q = jnp.einsum('bsr,rhd->bshd', rmsnorm(c_q, q_norm), w_uq)
    kv = jnp.einsum('bsr,rhd->bshd', rmsnorm(c_kv, kv_norm), w_ukv)
    cos, sin = rope_tables(s)
    q_pe = apply_rope(q[..., QK_NOPE:], cos[None, :, None, :], sin[None, :, None, :])
    k_pe = apply_rope(k_pe, cos[None], sin[None])
    scale = (QK_NOPE + QK_ROPE) ** -0.5
    qh = jnp.concatenate([q[..., :QK_NOPE], q_pe], axis=-1) * scale
    kh = jnp.concatenate([kv[..., :QK_NOPE], jnp.broadcast_to(k_pe[:, :, None, :], (bsz, s, MLA_HEADS, QK_ROPE))], axis=-1)
    v = kv[..., QK_NOPE:]
    qb = qh.reshape(bsz, s // Q_BLOCK, Q_BLOCK, MLA_HEADS, QK_NOPE + QK_ROPE).transpose(1, 0, 2, 3, 4)

    def block(qblk):
        sc = jnp.einsum('bqhd,bkhd->bhqk', qblk, kh).astype(jnp.float32)
        p = jax.nn.softmax(sc, axis=-1).astype(v.dtype)
        return jnp.einsum('bhqk,bkhd->bqhd', p, v)

    o = lax.map(block, qb)
    return o.transpose(1, 0, 2, 3, 4).reshape(bsz, s, MLA_WIDTH)


def s5_direction(uf, lam_re, lam_im, log_dt, b_re, b_im, c_re, c_im, reverse):
    f32 = jnp.float32
    lam_re = lam_re.astype(f32)
    lam_im = lam_im.astype(f32)
    dt = jnp.exp(log_dt.astype(f32))[:, None]
    mag = jnp.exp(lam_re * dt)
    a_re = mag * jnp.cos(lam_im * dt)
    a_im = mag * jnp.sin(lam_im * dt)
    nr = a_re - 1.0
    den = lam_re * lam_re + lam_im * lam_im
    f_re = (nr * lam_re + a_im * lam_im) / den
    f_im = (a_im * lam_re - nr * lam_im) / den
    b_re = b_re.astype(f32)
    b_im = b_im.astype(f32)
    bb_re = f_re[..., None] * b_re - f_im[..., None] * b_im
    bb_im = f_re[..., None] * b_im + f_im[..., None] * b_re
    bu_re = jnp.einsum('gph,bsgh->bsgp', bb_re, uf)
    bu_im = jnp.einsum('gph,bsgh->bsgp', bb_im, uf)
    ar = jnp.broadcast_to(a_re, bu_re.shape)
    ai = jnp.broadcast_to(a_im, bu_re.shape)

    def combine(e1, e2):
        a1r, a1i, b1r, b1i = e1
        a2r, a2i, b2r, b2i = e2
        return (a2r * a1r - a2i * a1i,
                a2r * a1i + a2i * a1r,
                a2r * b1r - a2i * b1i + b2r,
                a2r * b1i + a2i * b1r + b2i)

    _, _, s_re, s_im = lax.associative_scan(combine, (ar, ai, bu_re, bu_im), reverse=reverse, axis=1)
    return (jnp.einsum('ghp,bsgp->bsgh', c_re.astype(f32), s_re)
            - jnp.einsum('ghp,bsgp->bsgh', c_im.astype(f32), s_im))


def s5_group(u, lam_re, lam_im, log_dt, b_re, b_im, c_re, c_im, d_skip, w_glu, b_glu):
    bsz, s, _ = u.shape
    f32 = jnp.float32
    uf = u.astype(f32).reshape(bsz, s, SSM_GROUPS, SSM_GROUP_CH)
    y = d_skip.astype(f32) * uf
    for direction in range(2):
        y = y + s5_direction(uf, lam_re[direction], lam_im[direction], log_dt[direction],
                             b_re[direction], b_im[direction], c_re[direction], c_im[direction],
                             reverse=(direction == 1))
    z = jax.nn.gelu(y.reshape(bsz, s, SSM_WIDTH))
    z = z * jax.nn.sigmoid(z @ w_glu.astype(f32) + b_glu.astype(f32))
    return z.astype(u.dtype)


def peer(x, w_query, sub_keys, expert_u, expert_v):
    bsz, s, d = x.shape
    n_tok = bsz * s
    t = x.reshape(n_tok, d)
    q = jnp.einsum('td,de->te', t, w_query).reshape(n_tok, PEER_HEADS, 2, PEER_DKEY // 2)
    s1 = jnp.einsum('thc,hkc->thk', q[:, :, 0], sub_keys[0]).astype(jnp.float32)
    s2 = jnp.einsum('thc,hkc->thk', q[:, :, 1], sub_keys[1]).astype(jnp.float32)
    v1, i1 = lax.top_k(s1, PEER_TOPK)
    v2, i2 = lax.top_k(s2, PEER_TOPK)
    n_cand = PEER_TOPK * PEER_TOPK
    cand = (v1[..., :, None] + v2[..., None, :]).reshape(n_tok, PEER_HEADS, n_cand)
    cand_idx = (i1[..., :, None] * PEER_NKEYS + i2[..., None, :]).reshape(n_tok, PEER_HEADS, n_cand)
    top_s, pos = lax.top_k(cand, PEER_TOPK)
    idx = jnp.take_along_axis(cand_idx, pos, axis=-1)
    gate = jax.nn.softmax(top_s, axis=-1).astype(x.dtype)
    nb = n_tok // TOKEN_BLOCK
    xb = t.reshape(nb, TOKEN_BLOCK, d)
    ib = idx.reshape(nb, TOKEN_BLOCK, PEER_HEADS, PEER_TOPK)
    gb = gate.reshape(nb, TOKEN_BLOCK, PEER_HEADS, PEER_TOPK)

    def block(args):
        xt, it, gt = args
        a = jnp.einsum('td,thkd->thk', xt, jnp.take(expert_u, it, axis=0))
        hid = jax.nn.gelu(a) * gt
        return jnp.einsum('thk,thkd->td', hid, jnp.take(expert_v, it, axis=0))

    out = lax.map(block, (xb, ib, gb))
    return out.reshape(bsz, s, d)


def encoder(x, mix_norm, w_in, q_norm, kv_norm, w_uq, w_ukv, lam_re, lam_im, log_dt,
            b_re, b_im, c_re, c_im, d_skip, w_glu, b_glu, w_out, ffn_norm, w_query,
            sub_keys, expert_u, expert_v, final_norm):
    o1 = Q_LORA
    o2 = o1 + KV_LORA
    o3 = o2 + QK_ROPE
    for l in range(DEPTH):
        h = rmsnorm(x, mix_norm[l])
        proj = jnp.einsum('bsd,de->bse', h, w_in[l])
        att = mla_group(proj[..., :o1], proj[..., o1:o2], proj[..., o2:o3],
                        q_norm[l], kv_norm[l], w_uq[l], w_ukv[l])
        ssm = s5_group(proj[..., o3:], lam_re[l], lam_im[l], log_dt[l], b_re[l], b_im[l],
                       c_re[l], c_im[l], d_skip[l], w_glu[l], b_glu[l])
        x = x + jnp.einsum('bse,ed->bsd', jnp.concatenate([att, ssm], axis=-1), w_out[l])
        x = x + peer(rmsnorm(x, ffn_norm[l]), w_query[l], sub_keys[l], expert_u[l], expert_v[l])
    return rmsnorm(x, final_norm)


def setup_inputs(seed: int = 0) -> dict:
    key = jax.random.key(seed)
    ks = jax.random.split(key, 32)
    f32 = jnp.float32
    L, G, P, CH = DEPTH, SSM_GROUPS, SSM_STATE, SSM_GROUP_CH

    def nrm(k, shape, scale):
        return jax.random.normal(k, shape, f32) * scale

    def gain(k, shape):
        return 1.0 + 0.02 * jax.random.normal(k, shape, f32)

    n_idx = jnp.arange(P, dtype=f32)
    return {
        'x_prompt': nrm(ks[0], (BATCH, SEQ, D_MODEL), 1.0),
        'x_sample': nrm(ks[1], (DEC_BATCH, DEC_SEQ, D_MODEL), 1.0),
        'mix_norm': gain(ks[2], (L, D_MODEL)),
        'w_in': nrm(ks[3], (L, D_MODEL, IN_WIDTH), D_MODEL ** -0.5),
        'q_norm': gain(ks[4], (L, Q_LORA)),
        'kv_norm': gain(ks[5], (L, KV_LORA)),
        'w_uq': nrm(ks[6], (L, Q_LORA, MLA_HEADS, QK_NOPE + QK_ROPE), Q_LORA ** -0.5),
        'w_ukv': nrm(ks[7], (L, KV_LORA, MLA_HEADS, QK_NOPE + V_HEAD), KV_LORA ** -0.5),
        'lam_re': -0.5 + 0.01 * jax.random.normal(ks[8], (L, 2, G, P), f32),
        'lam_im': math.pi * n_idx + 0.01 * jax.random.normal(ks[9], (L, 2, G, P), f32),
        'log_dt': jax.random.uniform(ks[10], (L, 2, G), f32, math.log(DT_MIN), math.log(DT_MAX)),
        'b_re': nrm(ks[11], (L, 2, G, P, CH), (2.0 * CH) ** -0.5),
        'b_im': nrm(ks[12], (L, 2, G, P, CH), (2.0 * CH) ** -0.5),
        'c_re': nrm(ks[13], (L, 2, G, CH, P), (2.0 * P) ** -0.5),
        'c_im': nrm(ks[14], (L, 2, G, CH, P), (2.0 * P) ** -0.5),
        'd_skip': nrm(ks[15], (L, G, CH), 1.0),
        'w_glu': nrm(ks[16], (L, SSM_WIDTH, SSM_WIDTH), SSM_WIDTH ** -0.5),
        'b_glu': nrm(ks[17], (L, SSM_WIDTH), 0.02),
        'w_out': nrm(ks[18], (L, MIX_WIDTH, D_MODEL), MIX_WIDTH ** -0.5),
        'ffn_norm': gain(ks[19], (L, D_MODEL)),
        'w_query': nrm(ks[20], (L, D_MODEL, PEER_HEADS * PEER_DKEY), D_MODEL ** -0.5),
        'sub_keys': nrm(ks[21], (L, 2, PEER_HEADS, PEER_NKEYS, PEER_DKEY // 2), (PEER_DKEY // 2) ** -0.5),
        'expert_u': nrm(ks[22], (L, PEER_EXPERTS, D_MODEL), D_MODEL ** -0.5),
        'expert_v': nrm(ks[23], (L, PEER_EXPERTS, D_MODEL), PEER_TOPK ** -0.5),
        'final_norm': gain(ks[24], (D_MODEL,)),
    }


def reference(x_prompt, x_sample, mix_norm, w_in, q_norm, kv_norm, w_uq, w_ukv, lam_re, lam_im,
              log_dt, b_re, b_im, c_re, c_im, d_skip, w_glu, b_glu, w_out, ffn_norm, w_query,
              sub_keys, expert_u, expert_v, final_norm):
    weights = (mix_norm, w_in, q_norm, kv_norm, w_uq, w_ukv, lam_re, lam_im, log_dt,
               b_re, b_im, c_re, c_im, d_skip, w_glu, b_glu, w_out, ffn_norm, w_query,
               sub_keys, expert_u, expert_v, final_norm)
    y_prompt = encoder(x_prompt, *weights)
    y_sample = encoder(x_sample, *weights)
    return (y_prompt, y_sample)
```

```python
import functools
import math

import jax
import jax.numpy as jnp
from jax import lax
from jax.experimental import pallas as pl
from jax.experimental.pallas import tpu as pltpu

F32 = jnp.float32
BF16 = jnp.bfloat16

EPS = 1e-6
ROPE_THETA = 10000.0
LANES = 128
SUBLANES = 8
VMEM_LIMIT = 56 * 1024 * 1024

N_HEADS = 4
QK_NOPE = 128
QK_ROPE = 64
V_HEAD = 128
QK_PAD = 256
Q_LORA = 384
KV_LORA = 256
SSM_WIDTH = 512
SSM_GROUPS = 32
SSM_GROUP_CH = 16
SSM_STATE = 64
SSM_CHUNKS = SSM_GROUPS * SSM_STATE // LANES
PEER_HEADS = 8
PEER_NKEYS = 128
PEER_TOPK = 16
GELU_C = math.sqrt(2.0 / math.pi)


def _gelu(x):
    return 0.5 * x * (1.0 + jnp.tanh(GELU_C * (x + 0.044715 * (x * x * x))))


def _rms(x, g):
    return x * lax.rsqrt(jnp.mean(x * x, axis=-1, keepdims=True) + EPS) * g


def _params(*sem):
    return pltpu.CompilerParams(dimension_semantics=sem, vmem_limit_bytes=VMEM_LIMIT)


def _in_kernel(x_ref, cs_ref, g_ref, win_ref, qn_ref, kvn_ref, wq_ref, wkv_ref,
               q_ref, k_ref, v_ref, u_ref):
    x = x_ref[...]
    h = _rms(x, g_ref[...])
    proj = jnp.dot(h.astype(BF16), win_ref[...], preferred_element_type=F32)
    o1 = Q_LORA
    o2 = o1 + KV_LORA
    o3 = o2 + 2 * QK_ROPE
    cq = proj[:, :o1]
    ckv = proj[:, o1:o2]
    kp = proj[:, o2:o3]
    u_ref[...] = proj[:, o3:]
    cs = cs_ref[...]
    lane = lax.broadcasted_iota(jnp.int32, kp.shape, 1)
    t = kp * cs
    kpe = jnp.where(lane < QK_ROPE, t + pltpu.roll(t, QK_ROPE, axis=1), 0.0)
    scale = (QK_NOPE + QK_ROPE) ** -0.5
    q = jnp.dot(_rms(cq, qn_ref[...]).astype(BF16), wq_ref[...], preferred_element_type=F32) * scale
    kv = jnp.dot(_rms(ckv, kvn_ref[...]).astype(BF16), wkv_ref[...], preferred_element_type=F32)
    for hd in range(N_HEADS):
        b = hd * QK_PAD
        tq = q[:, b + QK_NOPE:b + QK_PAD] * cs
        q_ref[hd, :, :QK_NOPE] = q[:, b:b + QK_NOPE].astype(BF16)
        q_ref[hd, :, QK_NOPE:] = (tq + pltpu.roll(tq, QK_ROPE, axis=1)).astype(BF16)
        k_ref[hd, :, :QK_NOPE] = kv[:, b:b + QK_NOPE].astype(BF16)
        k_ref[hd, :, QK_NOPE:] = kpe.astype(BF16)
        v_ref[hd] = kv[:, b + QK_NOPE:b + QK_PAD].astype(BF16)


def _in_stage(x, cs, w, tt=512):
    bsz, s, d = x.shape
    tt = min(tt, s)
    grid = (bsz, s // tt)
    full = lambda a: pl.BlockSpec(a.shape, lambda b, i: (0,) * a.ndim)
    hs = lambda wd: pl.BlockSpec((None, N_HEADS, tt, wd), lambda b, i: (b, 0, i, 0))
    return pl.pallas_call(
        _in_kernel,
        grid=grid,
        in_specs=[pl.BlockSpec((None, tt, d), lambda b, i: (b, i, 0)),
                  pl.BlockSpec((tt, 2 * QK_ROPE), lambda b, i: (i, 0)),
                  full(w['mix_norm']), full(w['w_in']), full(w['q_norm']), full(w['kv_norm']),
                  full(w['w_q']), full(w['w_kv'])],
        out_specs=[hs(QK_PAD), hs(QK_PAD), hs(V_HEAD),
                   pl.BlockSpec((None, tt, SSM_WIDTH), lambda b, i: (b, i, 0))],
        out_shape=[jax.ShapeDtypeStruct((bsz, N_HEADS, s, QK_PAD), BF16),
                   jax.ShapeDtypeStruct((bsz, N_HEADS, s, QK_PAD), BF16),
                   jax.ShapeDtypeStruct((bsz, N_HEADS, s, V_HEAD), BF16),
                   jax.ShapeDtypeStruct((bsz, s, SSM_WIDTH), F32)],
        compiler_params=_params("parallel", "parallel"),
        name="in_stage",
    )(x, cs, w['mix_norm'], w['w_in'], w['q_norm'], w['kv_norm'], w['w_q'], w['w_kv'])


def _flash_kernel(q_ref, k_ref, v_ref, o_ref, m_ref, l_ref, acc_ref, *, tk):
    s_len = k_ref.shape[0]
    q = q_ref[...]
    m_ref[...] = jnp.full(m_ref.shape, -jnp.inf, F32)
    l_ref[...] = jnp.zeros(l_ref.shape, F32)
    acc_ref[...] = jnp.zeros(acc_ref.shape, F32)

    def body(j, carry):
        off = pl.multiple_of(j * tk, tk)
        kj = k_ref[pl.ds(off, tk), :]
        vj = v_ref[pl.ds(off, tk), :]
        s = lax.dot_general(q, kj, (((1,), (1,)), ((), ())), preferred_element_type=F32)
        m_prev = m_ref[...]
        m_new = jnp.maximum(m_prev, jnp.max(s, axis=1, keepdims=True))
        p = jnp.exp(s - m_new)
        alpha = jnp.exp(m_prev - m_new)
        l_ref[...] = alpha * l_ref[...] + jnp.sum(p, axis=1, keepdims=True)
        acc_ref[...] = alpha * acc_ref[...] + jnp.dot(p.astype(BF16), vj, preferred_element_type=F32)
        m_ref[...] = m_new
        return carry

    lax.fori_loop(0, s_len // tk, body, 0)
    o_ref[...] = (acc_ref[...] / l_ref[...]).astype(o_ref.dtype)


def _attention(q, k, v, tq=512, tk=512):
    bsz, nh, s, _ = q.shape
    tq = min(tq, s)
    tk = min(tk, s)
    return pl.pallas_call(
        functools.partial(_flash_kernel, tk=tk),
        grid=(bsz, nh, s // tq),
        in_specs=[pl.BlockSpec((None, None, tq, QK_PAD), lambda b, h, i: (b, h, i, 0)),
                  pl.BlockSpec((None, None, s, QK_PAD), lambda b, h, i: (b, h, 0, 0)),
                  pl.BlockSpec((None, None, s, V_HEAD), lambda b, h, i: (b, h, 0, 0))],
        out_specs=pl.BlockSpec((None, tq, V_HEAD), lambda b, h, i: (b, i, h)),
        out_shape=jax.ShapeDtypeStruct((bsz, s, nh * V_HEAD), BF16),
        scratch_shapes=[pltpu.VMEM((tq, 1), F32), pltpu.VMEM((tq, 1), F32),
                        pltpu.VMEM((tq, V_HEAD), F32)],
        compiler_params=_params("parallel", "parallel", "arbitrary"),
        name="attention",
    )(q, k, v)


def _s5_kernel(uf_ref, ub_ref, wf_ref, wb_ref, af_ref, ab_ref, cf_ref, cb_ref,
               yf_ref, yb_ref, fre, fim, bre, bim, st_ref, *, tc, pitch):
    half = SSM_CHUNKS // 2 * pitch

    @pl.when(pl.program_id(1) == 0)
    def _():
        st_ref[...] = jnp.zeros(st_ref.shape, F32)

    for u_ref, w_ref, re, im in ((uf_ref, wf_ref, fre, fim), (ub_ref, wb_ref, bre, bim)):
        for blk in range(SSM_WIDTH // LANES):
            ub = u_ref[:, blk * LANES:(blk + 1) * LANES].astype(BF16)
            for cc in range(SSM_CHUNKS * LANES // SSM_WIDTH):
                ch = blk * (SSM_CHUNKS * LANES // SSM_WIDTH) + cc
                r = jnp.dot(ub, w_ref[ch], preferred_element_type=F32)
                re[pl.ds(ch * pitch, tc), :] = r[:, :LANES]
                im[pl.ds(ch * pitch, tc), :] = r[:, LANES:]

    af = af_ref[...]
    ab = ab_ref[...]

    def step(a, st, re, im, t):
        out = []
        for hf in range(2):
            rows = pl.ds(hf * half + t, SUBLANES, stride=pitch)
            a_re, a_im = a[hf], a[2 + hf]
            s_re, s_im = st[hf], st[2 + hf]
            n_re = a_re * s_re - a_im * s_im + re[rows, :]
            n_im = a_re * s_im + a_im * s_re + im[rows, :]
            re[rows, :] = n_re
            im[rows, :] = n_im
            out.append((n_re, n_im))
        return (out[0][0], out[1][0], out[0][1], out[1][1])

    def body(t, carry):
        sf, sb = carry
        sf = step(af, sf, fre, fim, t)
        sb = step(ab, sb, bre, bim, tc - 1 - t)
        return (sf, sb)

    st = st_ref[...]
    init = (tuple(st[0, i] for i in range(4)), tuple(st[1, i] for i in range(4)))
    sf, sb = lax.fori_loop(0, tc, body, init, unroll=4)
    for i in range(4):
        st_ref[0, i] = sf[i]
        st_ref[1, i] = sb[i]

    for y_ref, c_ref, re, im in ((yf_ref, cf_ref, fre, fim), (yb_ref, cb_ref, bre, bim)):
        for blk in range(SSM_WIDTH // LANES):
            acc = None
            for cc in range(SSM_CHUNKS * LANES // SSM_WIDTH):
                ch = blk * (SSM_CHUNKS * LANES // SSM_WIDTH) + cc
                s_re = re[pl.ds(ch * pitch, tc), :].astype(BF16)
                s_im = im[pl.ds(ch * pitch, tc), :].astype(BF16)
                d = (jnp.dot(s_re, c_ref[0, ch], preferred_element_type=F32)
                     + jnp.dot(s_im, c_ref[1, ch], preferred_element_type=F32))
                acc = d if acc is None else acc + d
            y_ref[:, blk * LANES:(blk + 1) * LANES] = acc


def _s5_scan(u, w, tc=256):
    bsz, s, _ = u.shape
    tc = min(tc, s)
    n = s // tc
    pitch = tc + SUBLANES
    rows = SSM_CHUNKS * pitch
    full = lambda a: pl.BlockSpec(a.shape, lambda b, c: (0,) * a.ndim)
    ublk = lambda f: pl.BlockSpec((None, tc, SSM_WIDTH), f)
    fwd = lambda b, c: (b, c, 0)
    bwd = lambda b, c: (b, n - 1 - c, 0)
    return pl.pallas_call(
        functools.partial(_s5_kernel, tc=tc, pitch=pitch),
        grid=(bsz, n),
        in_specs=[ublk(fwd), ublk(bwd), full(w['s5_wf']), full(w['s5_wb']),
                  full(w['s5_af']), full(w['s5_ab']), full(w['s5_cf']), full(w['s5_cb'])],
        out_specs=[ublk(fwd), ublk(bwd)],
        out_shape=[jax.ShapeDtypeStruct(u.shape, F32), jax.ShapeDtypeStruct(u.shape, F32)],
        scratch_shapes=[pltpu.VMEM((rows, LANES), F32) for _ in range(4)]
                       + [pltpu.VMEM((2, 4, SUBLANES, LANES), F32)],
        compiler_params=_params("parallel", "arbitrary"),
        name="s5_scan",
    )(u, u, w['s5_wf'], w['s5_wb'], w['s5_af'], w['s5_ab'], w['s5_cf'], w['s5_cb'])


def _mid_kernel(x_ref, att_ref, yf_ref, yb_ref, u_ref, dsk_ref, wglu_ref, bglu_ref,
                woa_ref, wos_ref, fn_ref, wqry_ref, x1_ref, xn_ref, qp_ref):
    y = dsk_ref[...] * u_ref[...] + yf_ref[...] + yb_ref[...]
    z = _gelu(y)
    gate = jnp.dot(z.astype(BF16), wglu_ref[...], preferred_element_type=F32) + bglu_ref[...]
    z = z * (1.0 / (1.0 + jnp.exp(-gate)))
    mix = (jnp.dot(att_ref[...], woa_ref[...], preferred_element_type=F32)
           + jnp.dot(z.astype(BF16), wos_ref[...], preferred_element_type=F32))
    x1 = x_ref[...] + mix
    x1_ref[...] = x1
    xn = _rms(x1, fn_ref[...]).astype(BF16)
    xn_ref[...] = xn
    qp_ref[...] = jnp.dot(xn, wqry_ref[...], preferred_element_type=F32).astype(BF16)


def _mid_stage(x, att, yf, yb, u, w, tt=512):
    n, d = x.shape
    tt = min(tt, n)
    full = lambda a: pl.BlockSpec(a.shape, lambda i: (0,) * a.ndim)
    row = lambda wd: pl.BlockSpec((tt, wd), lambda i: (i, 0))
    nq = w['w_query'].shape[1]
    return pl.pallas_call(
        _mid_kernel,
        grid=(n // tt,),
        in_specs=[row(d), row(att.shape[1]), row(SSM_WIDTH), row(SSM_WIDTH), row(SSM_WIDTH),
                  full(w['d_skip']), full(w['w_glu']), full(w['b_glu']), full(w['w_out_att']),
                  full(w['w_out_ssm']), full(w['ffn_norm']), full(w['w_query'])],
        out_specs=[row(d), row(d), row(nq)],
        out_shape=[jax.ShapeDtypeStruct((n, d), F32), jax.ShapeDtypeStruct((n, d), BF16),
                   jax.ShapeDtypeStruct((n, nq), BF16)],
        compiler_params=_params("parallel"),
        name="mid_stage",
    )(x, att, yf, yb, u, w['d_skip'], w['w_glu'], w['b_glu'], w['w_out_att'], w['w_out_ssm'],
      w['ffn_norm'], w['w_query'])


def _merge_desc(x):
    n = len(x)
    if n == 1:
        return x
    h = n // 2
    hi = [jnp.maximum(x[i], x[i + h]) for i in range(h)]
    lo = [jnp.minimum(x[i], x[i + h]) for i in range(h)]
    return _merge_desc(hi) + _merge_desc(lo)


def _sort_desc(x):
    n = len(x)
    if n == 1:
        return x
    h = n // 2
    return _merge_desc(_sort_desc(x[:h]) + _sort_desc(x[h:])[::-1])


def _merge_top(a, b):
    k = len(a)
    c = [jnp.maximum(a[i], b[k - 1 - i]) if k - 1 - i < len(b) else a[i] for i in range(k)]
    return _merge_desc(c)


def _route_kernel(qp_ref, keys_ref, n1_ref, e1_ref, r2_ref, e2_ref, sc_ref, out_ref, *, pitch):
    k_top = PEER_TOPK
    nk = PEER_NKEYS
    nsub = qp_ref.shape[0] // LANES

    for side in range(2):
        kmat = keys_ref[side]
        for j in range(nsub):
            qj = qp_ref[j * LANES:(j + 1) * LANES, side * LANES:(side + 1) * LANES]
            sc_ref[side, pl.ds(j * pitch, nk), :] = lax.dot_general(
                kmat, qj, (((1,), (1,)), ((), ())), preferred_element_type=F32)

    def key_rows(kk):
        return pl.ds(kk, nsub, stride=pitch)

    def top_sorted(side):
        best = None
        for g in range(nk // k_top):
            grp = _sort_desc([sc_ref[side, key_rows(g * k_top + i), :] for i in range(k_top)])
            best = grp if best is None else _merge_top(best, grp)
        return best

    v1 = top_sorted(0)
    v2 = top_sorted(1)

    lens = [k_top // (a + 1) for a in range(k_top)]
    cand = [[v1[a] + v2[b] for b in range(lens[a])] for a in range(k_top)]
    top = cand[0]
    a = 1
    while lens[a] > 1:
        top = _merge_top(top, cand[a])
        a += 1
    top = _merge_top(top, [cand[i][0] for i in range(a, k_top)])
    thr = top[k_top - 1]

    m = cand[0][0]
    z = None
    for row in cand:
        for c in row:
            e = jnp.where(c >= thr, jnp.exp(c - m), 0.0)
            z = e if z is None else z + e
    inv_z = 1.0 / z

    inf = jnp.full(thr.shape, jnp.inf, F32)
    tb = []
    for b in range(k_top):
        t = inf
        for a in range(k_top):
            if b < lens[a]:
                t = jnp.minimum(t, jnp.where(cand[a][b] >= thr, v1[a], inf))
        tb.append(t)

    for kk in range(nk):
        rows = key_rows(kk)
        s1 = sc_ref[0, rows, :]
        cnt = jnp.zeros_like(s1)
        for b in range(k_top):
            cnt = jnp.where(s1 >= tb[b], float(b + 1), cnt)
        out_ref[0, rows, :] = cnt
        out_ref[1, rows, :] = jnp.exp(s1 - v1[0]) * inv_z
        s2 = sc_ref[1, rows, :]
        rank = jnp.full(s2.shape, float(k_top), F32)
        for b in range(k_top - 1, -1, -1):
            rank = jnp.where(s2 >= v2[b], float(b), rank)
        out_ref[2, rows, :] = rank
        out_ref[3, rows, :] = jnp.exp(s2 - v2[0])

    for j in range(nsub):
        cols = slice(j * LANES, (j + 1) * LANES)
        rows = pl.ds(j * pitch, nk)
        n1_ref[:, cols] = out_ref[0, rows, :]
        e1_ref[:, cols] = out_ref[1, rows, :]
        r2_ref[:, cols] = out_ref[2, rows, :].astype(BF16)
        e2_ref[:, cols] = out_ref[3, rows, :].astype(BF16)


def _route(qp, keys):
    n = qp.shape[0]
    tt = min(SUBLANES * LANES, n)
    nsub = tt // LANES
    pitch = PEER_NKEYS + SUBLANES
    rows = nsub * pitch
    hk = 2 * LANES
    out_blk = pl.BlockSpec((None, PEER_NKEYS, tt), lambda i, h: (h, 0, i))
    shp = lambda dt: jax.ShapeDtypeStruct((PEER_HEADS, PEER_NKEYS, n), dt)
    return pl.pallas_call(
        functools.partial(_route_kernel, pitch=pitch),
        grid=(n // tt, PEER_HEADS),
        in_specs=[pl.BlockSpec((tt, hk), lambda i, h: (i, h)),
                  pl.BlockSpec((2, None, PEER_NKEYS, LANES), lambda i, h: (0, h, 0, 0))],
        out_specs=[out_blk, out_blk, out_blk, out_blk],
        out_shape=[shp(F32), shp(F32), shp(BF16), shp(BF16)],
        scratch_shapes=[pltpu.VMEM((2, rows, LANES), F32), pltpu.VMEM((4, rows, LANES), F32)],
        compiler_params=_params("parallel", "parallel"),
        name="peer_route",
    )(qp, keys)


def _peer_kernel(xn_ref, x1_ref, n1_ref, e1_ref, r2_ref, e2_ref, u_ref, vt_ref, fn_ref,
                 y_ref, a_ref, hid_ref, acc_ref, *, tcol):
    e = pl.program_id(1)
    tt = xn_ref.shape[0]
    rows_per_tile = u_ref.shape[0] // PEER_NKEYS

    @pl.when(e == 0)
    def _():
        acc_ref[...] = jnp.zeros(acc_ref.shape, F32)

    a_ref[...] = lax.dot_general(u_ref[...], xn_ref[...], (((1,), (1,)), ((), ())),
                                 preferred_element_type=F32)

    for i1 in range(rows_per_tile):
        row = e * rows_per_tile + i1
        for c in range(tt // tcol):
            cols = slice(c * tcol, (c + 1) * tcol)
            g = None
            for h in range(PEER_HEADS):
                n1 = jnp.broadcast_to(n1_ref[h, pl.ds(row, 1), cols].astype(BF16), (PEER_NKEYS, tcol))
                e1 = jnp.broadcast_to(e1_ref[h, pl.ds(row, 1), cols].astype(BF16), (PEER_NKEYS, tcol))
                sel = jnp.where(r2_ref[h, :, cols] < n1, e2_ref[h, :, cols], jnp.zeros((), BF16)) * e1
                g = sel if g is None else g + sel
            a = a_ref[i1 * PEER_NKEYS:(i1 + 1) * PEER_NKEYS, cols].astype(BF16)
            hid_ref[i1 * PEER_NKEYS:(i1 + 1) * PEER_NKEYS, cols] = _gelu(a) * g

    acc_ref[...] += jnp.dot(vt_ref[...], hid_ref[...], preferred_element_type=F32)

    @pl.when(e == pl.num_programs(1) - 1)
    def _():
        x2 = x1_ref[...] + acc_ref[...].T
        y_ref[...] = _rms(x2, fn_ref[...])


def _peer(xn, x1, n1, e1, r2, e2, w, tt=512, te=1024, tcol=256):
    n, d = xn.shape
    tt = min(tt, n)
    tcol = min(tcol, tt)
    ne = w['expert_u'].shape[0]
    rt = lambda: pl.BlockSpec((PEER_HEADS, PEER_NKEYS, tt), lambda i, e: (0, 0, i))
    return pl.pallas_call(
        functools.partial(_peer_kernel, tcol=tcol),
        grid=(n // tt, ne // te),
        in_specs=[pl.BlockSpec((tt, d), lambda i, e: (i, 0)),
                  pl.BlockSpec((tt, d), lambda i, e: (i, 0)),
                  rt(), rt(), rt(), rt(),
                  pl.BlockSpec((te, d), lambda i, e: (e, 0)),
                  pl.BlockSpec((d, te), lambda i, e: (0, e)),
                  pl.BlockSpec((1, d), lambda i, e: (0, 0))],
        out_specs=pl.BlockSpec((tt, d), lambda i, e: (i, 0)),
        out_shape=jax.ShapeDtypeStruct((n, d), F32),
        scratch_shapes=[pltpu.VMEM((te, tt), F32), pltpu.VMEM((te, tt), BF16),
                        pltpu.VMEM((d, tt), F32)],
        compiler_params=_params("parallel", "arbitrary"),
        name="peer_experts",
    )(xn, x1, n1, e1, r2, e2, w['expert_u'], w['expert_vt'], w['final_norm'])


def _rot_cols(wm):
    hw = wm.shape[-1] // 2
    return jnp.concatenate([-wm[..., hw:], wm[..., :hw]], axis=-1)


def _s5_direction_params(lam_re, lam_im, log_dt, b_re, b_im, c_re, c_im):
    g, p, ch = SSM_GROUPS, SSM_STATE, SSM_GROUP_CH
    dt = jnp.exp(log_dt)[:, None]
    mag = jnp.exp(lam_re * dt)
    a_re = mag * jnp.cos(lam_im * dt)
    a_im = mag * jnp.sin(lam_im * dt)
    nr = a_re - 1.0
    den = lam_re * lam_re + lam_im * lam_im
    f_re = (nr * lam_re + a_im * lam_im) / den
    f_im = (a_im * lam_re - nr * lam_im) / den
    bb_re = f_re[..., None] * b_re - f_im[..., None] * b_im
    bb_im = f_re[..., None] * b_im + f_im[..., None] * b_re
    eye = jnp.eye(g, dtype=F32)
    per_blk = SSM_CHUNKS * LANES // SSM_WIDTH

    def expand(bb):
        m = jnp.einsum('gph,gk->ghkp', bb, eye).reshape(SSM_WIDTH // LANES, LANES, SSM_CHUNKS, LANES)
        return jnp.stack([m[c // per_blk, :, c, :] for c in range(SSM_CHUNKS)])

    def readout(cm):
        m = jnp.einsum('ghp,gk->gpkh', cm, eye).reshape(SSM_CHUNKS, LANES, SSM_WIDTH // LANES, LANES)
        return jnp.stack([m[c, :, c // per_blk, :] for c in range(SSM_CHUNKS)])

    wexp = jnp.concatenate([expand(bb_re), expand(bb_im)], axis=-1).astype(BF16)
    cmat = jnp.stack([readout(c_re), readout(-c_im)]).astype(BF16)
    a = jnp.concatenate([a_re.reshape(2, SUBLANES, LANES), a_im.reshape(2, SUBLANES, LANES)])
    return wexp, a, cmat


def _prepare(mix_norm, w_in, q_norm, kv_norm, w_uq, w_ukv, lam_re, lam_im, log_dt, b_re, b_im,
             c_re, c_im, d_skip, w_glu, b_glu, w_out, ffn_norm, w_query, sub_keys, expert_u,
             expert_v, final_norm):
    o1 = Q_LORA
    o2 = o1 + KV_LORA
    o3 = o2 + QK_ROPE
    wi = w_in[0]
    w_kpe = wi[:, o2:o3]
    w = {}
    w['mix_norm'] = mix_norm[0][None, :]
    w['w_in'] = jnp.concatenate([wi[:, :o2], w_kpe, _rot_cols(w_kpe), wi[:, o3:]], axis=1).astype(BF16)
    w['q_norm'] = q_norm[0][None, :]
    w['kv_norm'] = kv_norm[0][None, :]
    wq = w_uq[0]
    wq_pe = wq[..., QK_NOPE:]
    w['w_q'] = jnp.concatenate([wq[..., :QK_NOPE], wq_pe, _rot_cols(wq_pe)], axis=-1).reshape(
        Q_LORA, N_HEADS * QK_PAD).astype(BF16)
    w['w_kv'] = w_ukv[0].reshape(KV_LORA, N_HEADS * (QK_NOPE + V_HEAD)).astype(BF16)
    for name, d in (('f', 0), ('b', 1)):
        wexp, a, cmat = _s5_direction_params(lam_re[0, d], lam_im[0, d], log_dt[0, d], b_re[0, d],
                                             b_im[0, d], c_re[0, d], c_im[0, d])
        w['s5_w' + name], w['s5_a' + name], w['s5_c' + name] = wexp, a, cmat
    w['d_skip'] = d_skip[0].reshape(1, SSM_WIDTH)
    w['w_glu'] = w_glu[0].astype(BF16)
    w['b_glu'] = b_glu[0][None, :]
    mla_w = N_HEADS * V_HEAD
    w['w_out_att'] = w_out[0][:mla_w].astype(BF16)
    w['w_out_ssm'] = w_out[0][mla_w:].astype(BF16)
    w['ffn_norm'] = ffn_norm[0][None, :]
    w['w_query'] = w_query[0].astype(BF16)
    w['sub_keys'] = sub_keys[0].astype(BF16)
    w['expert_u'] = expert_u[0].astype(BF16)
    w['expert_vt'] = expert_v[0].astype(BF16).T
    w['final_norm'] = final_norm[None, :]
    return w


def _rope_table(s):
    inv = ROPE_THETA ** (-jnp.arange(0, QK_ROPE, 2, dtype=F32) / QK_ROPE)
    ang = jnp.arange(s, dtype=F32)[:, None] * inv[None, :]
    c, sn = jnp.cos(ang), jnp.sin(ang)
    return jnp.concatenate([c, c, sn, sn], axis=1)


def _encoder(x, w):
    bsz, s, d = x.shape
    q, k, v, u = _in_stage(x, _rope_table(s), w)
    att = _attention(q, k, v)
    yf, yb = _s5_scan(u, w)
    n = bsz * s
    x1, xn, qp = _mid_stage(x.reshape(n, d), att.reshape(n, -1), yf.reshape(n, -1),
                            yb.reshape(n, -1), u.reshape(n, -1), w)
    n1, e1, r2, e2 = _route(qp, w['sub_keys'])
    y = _peer(xn, x1, n1, e1, r2, e2, w)
    return y.reshape(bsz, s, d)


def kernel(x_prompt, x_sample, mix_norm, w_in, q_norm, kv_norm, w_uq, w_ukv, lam_re, lam_im, log_dt, b_re, b_im, c_re, c_im, d_skip, w_glu, b_glu, w_out, ffn_norm, w_query, sub_keys, expert_u, expert_v, final_norm):
    w = _prepare(mix_norm, w_in, q_norm, kv_norm, w_uq, w_ukv, lam_re, lam_im, log_dt, b_re, b_im,
                 c_re, c_im, d_skip, w_glu, b_glu, w_out, ffn_norm, w_query, sub_keys, expert_u,
                 expert_v, final_norm)
    return (_encoder(x_prompt, w), _encoder(x_sample, w))
```

```python
import functools
import math

import jax
import jax.numpy as jnp
from jax import lax
from jax.experimental import pallas as pl
from jax.experimental.pallas import tpu as pltpu

F32 = jnp.float32
BF16 = jnp.bfloat16

EPS = 1e-6
ROPE_THETA = 10000.0
LANES = 128
SUBLANES = 8
VMEM_LIMIT = 56 * 1024 * 1024

N_HEADS = 4
QK_NOPE = 128
QK_ROPE = 64
V_HEAD = 128
QK_PAD = 256
Q_LORA = 384
KV_LORA = 256
SSM_WIDTH = 512
SSM_GROUPS = 32
SSM_GROUP_CH = 16
SSM_STATE = 64
SSM_CHUNKS = SSM_GROUPS * SSM_STATE // LANES
PEER_HEADS = 8
PEER_NKEYS = 128
PEER_TOPK = 16
GELU_C = math.sqrt(2.0 / math.pi)


def _gelu(x):
    return 0.5 * x * (1.0 + jnp.tanh(GELU_C * (x + 0.044715 * (x * x * x))))


def _rms(x, g):
    return x * lax.rsqrt(jnp.mean(x * x, axis=-1, keepdims=True) + EPS) * g


def _params(*sem):
    return pltpu.CompilerParams(dimension_semantics=sem, vmem_limit_bytes=VMEM_LIMIT)


def _in_kernel(x_ref, cs_ref, g_ref, win_ref, qn_ref, kvn_ref, wq_ref, wkv_ref,
               q_ref, k_ref, v_ref, u_ref):
    x = x_ref[...]
    h = _rms(x, g_ref[...])
    proj = jnp.dot(h.astype(BF16), win_ref[...], preferred_element_type=F32)
    o1 = Q_LORA
    o2 = o1 + KV_LORA
    o3 = o2 + 2 * QK_ROPE
    cq = proj[:, :o1]
    ckv = proj[:, o1:o2]
    kp = proj[:, o2:o3]
    u_ref[...] = proj[:, o3:]
    cs = cs_ref[...]
    lane = lax.broadcasted_iota(jnp.int32, kp.shape, 1)
    t = kp * cs
    kpe = jnp.where(lane < QK_ROPE, t + pltpu.roll(t, QK_ROPE, axis=1), 0.0)
    scale = (QK_NOPE + QK_ROPE) ** -0.5
    q = jnp.dot(_rms(cq, qn_ref[...]).astype(BF16), wq_ref[...], preferred_element_type=F32) * scale
    kv = jnp.dot(_rms(ckv, kvn_ref[...]).astype(BF16), wkv_ref[...], preferred_element_type=F32)
    for hd in range(N_HEADS):
        b = hd * QK_PAD
        tq = q[:, b + QK_NOPE:b + QK_PAD] * cs
        q_ref[hd, :, :QK_NOPE] = q[:, b:b + QK_NOPE].astype(BF16)
        q_ref[hd, :, QK_NOPE:] = (tq + pltpu.roll(tq, QK_ROPE, axis=1)).astype(BF16)
        k_ref[hd, :, :QK_NOPE] = kv[:, b:b + QK_NOPE].astype(BF16)
        k_ref[hd, :, QK_NOPE:] = kpe.astype(BF16)
        v_ref[hd] = kv[:, b + QK_NOPE:b + QK_PAD].astype(BF16)


def _in_stage(x, cs, w, tt=512):
    bsz, s, d = x.shape
    tt = min(tt, s)
    grid = (bsz, s // tt)
    full = lambda a: pl.BlockSpec(a.shape, lambda b, i: (0,) * a.ndim)
    hs = lambda wd: pl.BlockSpec((None, N_HEADS, tt, wd), lambda b, i: (b, 0, i, 0))
    return pl.pallas_call(
        _in_kernel,
        grid=grid,
        in_specs=[pl.BlockSpec((None, tt, d), lambda b, i: (b, i, 0)),
                  pl.BlockSpec((tt, 2 * QK_ROPE), lambda b, i: (i, 0)),
                  full(w['mix_norm']), full(w['w_in']), full(w['q_norm']), full(w['kv_norm']),
                  full(w['w_q']), full(w['w_kv'])],
        out_specs=[hs(QK_PAD), hs(QK_PAD), hs(V_HEAD),
                   pl.BlockSpec((None, tt, SSM_WIDTH), lambda b, i: (b, i, 0))],
        out_shape=[jax.ShapeDtypeStruct((bsz, N_HEADS, s, QK_PAD), BF16),
                   jax.ShapeDtypeStruct((bsz, N_HEADS, s, QK_PAD), BF16),
                   jax.ShapeDtypeStruct((bsz, N_HEADS, s, V_HEAD), BF16),
                   jax.ShapeDtypeStruct((bsz, s, SSM_WIDTH), F32)],
        compiler_params=_params("parallel", "parallel"),
        name="in_stage",
    )(x, cs, w['mix_norm'], w['w_in'], w['q_norm'], w['kv_norm'], w['w_q'], w['w_kv'])


def _flash_kernel(q_ref, k_ref, v_ref, o_ref, m_ref, acc_ref, s_buf, p_buf, al_buf, *, tk):
    n = k_ref.shape[0] // tk
    m_ref[...] = jnp.full(m_ref.shape, -jnp.inf, F32)
    acc_ref[...] = jnp.zeros(acc_ref.shape, F32)
    ones = jnp.ones((tk, LANES), BF16)

    def scores(j, slot):
        kj = k_ref[pl.ds(pl.multiple_of(j * tk, tk), tk), :]
        s_buf[slot] = lax.dot_general(q_ref[...], kj, (((1,), (1,)), ((), ())),
                                      preferred_element_type=F32)

    def exps(slot):
        s = s_buf[slot]
        m_prev = m_ref[...]
        m_new = jnp.maximum(m_prev, jnp.max(s, axis=1, keepdims=True))
        p_buf[slot] = jnp.exp(s - pltpu.repeat(m_new, tk // LANES, axis=1)).astype(BF16)
        al_buf[slot] = jnp.exp(m_prev - m_new)
        m_ref[...] = m_new

    def values(j, slot):
        vj = jnp.concatenate([v_ref[pl.ds(pl.multiple_of(j * tk, tk), tk), :], ones], axis=1)
        acc_ref[...] = (pltpu.repeat(al_buf[slot], 2, axis=1) * acc_ref[...]
                        + jnp.dot(p_buf[slot], vj, preferred_element_type=F32))

    scores(0, 0)
    scores(1, 1)
    exps(0)

    def body(jj, carry):
        j = 2 * jj + 2
        scores(j, 0)
        exps(1)
        values(j - 2, 0)
        scores(j + 1, 1)
        exps(0)
        values(j - 1, 1)
        return carry

    lax.fori_loop(0, (n - 2) // 2, body, 0)
    exps(1)
    values(n - 2, 0)
    values(n - 1, 1)
    o_ref[...] = (acc_ref[:, :V_HEAD] / acc_ref[:, V_HEAD:]).astype(o_ref.dtype)


def _attention(q, k, v, tq=512, tk=512):
    bsz, nh, s, _ = q.shape
    tq = min(tq, s)
    tk = min(tk, s // 2)
    assert s % (2 * tk) == 0 and s % tq == 0
    return pl.pallas_call(
        functools.partial(_flash_kernel, tk=tk),
        grid=(bsz, nh, s // tq),
        in_specs=[pl.BlockSpec((None, None, tq, QK_PAD), lambda b, h, i: (b, h, i, 0)),
                  pl.BlockSpec((None, None, s, QK_PAD), lambda b, h, i: (b, h, 0, 0)),
                  pl.BlockSpec((None, None, s, V_HEAD), lambda b, h, i: (b, h, 0, 0))],
        out_specs=pl.BlockSpec((None, tq, V_HEAD), lambda b, h, i: (b, i, h)),
        out_shape=jax.ShapeDtypeStruct((bsz, s, nh * V_HEAD), BF16),
        scratch_shapes=[pltpu.VMEM((tq, LANES), F32), pltpu.VMEM((tq, 2 * V_HEAD), F32),
                        pltpu.VMEM((2, tq, tk), F32), pltpu.VMEM((2, tq, tk), BF16),
                        pltpu.VMEM((2, tq, LANES), F32)],
        compiler_params=_params("parallel", "parallel", "arbitrary"),
        name="attention",
    )(q, k, v)


def _s5_kernel(uf_ref, ub_ref, wf_ref, wb_ref, af_ref, ab_ref, cf_ref, cb_ref,
               yf_ref, yb_ref, fre, fim, bre, bim, st_ref, *, tc, pitch):
    half = SSM_CHUNKS // 2 * pitch

    @pl.when(pl.program_id(1) == 0)
    def _():
        st_ref[...] = jnp.zeros(st_ref.shape, F32)

    for u_ref, w_ref, re, im in ((uf_ref, wf_ref, fre, fim), (ub_ref, wb_ref, bre, bim)):
        for blk in range(SSM_WIDTH // LANES):
            ub = u_ref[:, blk * LANES:(blk + 1) * LANES].astype(BF16)
            for cc in range(SSM_CHUNKS * LANES // SSM_WIDTH):
                ch = blk * (SSM_CHUNKS * LANES // SSM_WIDTH) + cc
                r = jnp.dot(ub, w_ref[ch], preferred_element_type=F32)
                re[pl.ds(ch * pitch, tc), :] = r[:, :LANES]
                im[pl.ds(ch * pitch, tc), :] = r[:, LANES:]

    af = af_ref[...]
    ab = ab_ref[...]

    def step(a, st, re, im, t):
        out = []
        for hf in range(2):
            rows = pl.ds(hf * half + t, SUBLANES, stride=pitch)
            a_re, a_im = a[hf], a[2 + hf]
            s_re, s_im = st[hf], st[2 + hf]
            n_re = a_re * s_re - a_im * s_im + re[rows, :]
            n_im = a_re * s_im + a_im * s_re + im[rows, :]
            re[rows, :] = n_re
            im[rows, :] = n_im
            out.append((n_re, n_im))
        return (out[0][0], out[1][0], out[0][1], out[1][1])

    def body(t, carry):
        sf, sb = carry
        sf = step(af, sf, fre, fim, t)
        sb = step(ab, sb, bre, bim, tc - 1 - t)
        return (sf, sb)

    st = st_ref[...]
    init = (tuple(st[0, i] for i in range(4)), tuple(st[1, i] for i in range(4)))
    sf, sb = lax.fori_loop(0, tc, body, init, unroll=4)
    for i in range(4):
        st_ref[0, i] = sf[i]
        st_ref[1, i] = sb[i]

    for y_ref, c_ref, re, im in ((yf_ref, cf_ref, fre, fim), (yb_ref, cb_ref, bre, bim)):
        for blk in range(SSM_WIDTH // LANES):
            acc = None
            for cc in range(SSM_CHUNKS * LANES // SSM_WIDTH):
                ch = blk * (SSM_CHUNKS * LANES // SSM_WIDTH) + cc
                s_re = re[pl.ds(ch * pitch, tc), :].astype(BF16)
                s_im = im[pl.ds(ch * pitch, tc), :].astype(BF16)
                d = (jnp.dot(s_re, c_ref[0, ch], preferred_element_type=F32)
                     + jnp.dot(s_im, c_ref[1, ch], preferred_element_type=F32))
                acc = d if acc is None else acc + d
            y_ref[:, blk * LANES:(blk + 1) * LANES] = acc


def _s5_scan(u, w, tc=256):
    bsz, s, _ = u.shape
    tc = min(tc, s)
    n = s // tc
    pitch = tc + SUBLANES
    rows = SSM_CHUNKS * pitch
    full = lambda a: pl.BlockSpec(a.shape, lambda b, c: (0,) * a.ndim)
    ublk = lambda f: pl.BlockSpec((None, tc, SSM_WIDTH), f)
    fwd = lambda b, c: (b, c, 0)
    bwd = lambda b, c: (b, n - 1 - c, 0)
    return pl.pallas_call(
        functools.partial(_s5_kernel, tc=tc, pitch=pitch),
        grid=(bsz, n),
        in_specs=[ublk(fwd), ublk(bwd), full(w['s5_wf']), full(w['s5_wb']),
                  full(w['s5_af']), full(w['s5_ab']), full(w['s5_cf']), full(w['s5_cb'])],
        out_specs=[ublk(fwd), ublk(bwd)],
        out_shape=[jax.ShapeDtypeStruct(u.shape, F32), jax.ShapeDtypeStruct(u.shape, F32)],
        scratch_shapes=[pltpu.VMEM((rows, LANES), F32) for _ in range(4)]
                       + [pltpu.VMEM((2, 4, SUBLANES, LANES), F32)],
        compiler_params=_params("parallel", "arbitrary"),
        name="s5_scan",
    )(u, u, w['s5_wf'], w['s5_wb'], w['s5_af'], w['s5_ab'], w['s5_cf'], w['s5_cb'])


def _mid_kernel(x_ref, att_ref, yf_ref, yb_ref, u_ref, dsk_ref, wglu_ref, bglu_ref,
                woa_ref, wos_ref, fn_ref, wqry_ref, x1_ref, xn_ref, qp_ref):
    y = dsk_ref[...] * u_ref[...] + yf_ref[...] + yb_ref[...]
    z = _gelu(y)
    gate = jnp.dot(z.astype(BF16), wglu_ref[...], preferred_element_type=F32) + bglu_ref[...]
    z = z * (1.0 / (1.0 + jnp.exp(-gate)))
    mix = (jnp.dot(att_ref[...], woa_ref[...], preferred_element_type=F32)
           + jnp.dot(z.astype(BF16), wos_ref[...], preferred_element_type=F32))
    x1 = x_ref[...] + mix
    x1_ref[...] = x1
    xn = _rms(x1, fn_ref[...]).astype(BF16)
    xn_ref[...] = xn
    qp_ref[...] = jnp.dot(xn, wqry_ref[...], preferred_element_type=F32).astype(BF16)


def _mid_stage(x, att, yf, yb, u, w, tt=512):
    n, d = x.shape
    tt = min(tt, n)
    full = lambda a: pl.BlockSpec(a.shape, lambda i: (0,) * a.ndim)
    row = lambda wd: pl.BlockSpec((tt, wd), lambda i: (i, 0))
    nq = w['w_query'].shape[1]
    return pl.pallas_call(
        _mid_kernel,
        grid=(n // tt,),
        in_specs=[row(d), row(att.shape[1]), row(SSM_WIDTH), row(SSM_WIDTH), row(SSM_WIDTH),
                  full(w['d_skip']), full(w['w_glu']), full(w['b_glu']), full(w['w_out_att']),
                  full(w['w_out_ssm']), full(w['ffn_norm']), full(w['w_query'])],
        out_specs=[row(d), row(d), row(nq)],
        out_shape=[jax.ShapeDtypeStruct((n, d), F32), jax.ShapeDtypeStruct((n, d), BF16),
                   jax.ShapeDtypeStruct((n, nq), BF16)],
        compiler_params=_params("parallel"),
        name="mid_stage",
    )(x, att, yf, yb, u, w['d_skip'], w['w_glu'], w['b_glu'], w['w_out_att'], w['w_out_ssm'],
      w['ffn_norm'], w['w_query'])


def _merge_desc(x):
    n = len(x)
    if n == 1:
        return x
    h = n // 2
    hi = [jnp.maximum(x[i], x[i + h]) for i in range(h)]
    lo = [jnp.minimum(x[i], x[i + h]) for i in range(h)]
    return _merge_desc(hi) + _merge_desc(lo)


def _sort_desc(x):
    n = len(x)
    if n == 1:
        return x
    h = n // 2
    return _merge_desc(_sort_desc(x[:h]) + _sort_desc(x[h:])[::-1])


def _merge_top(a, b):
    k = len(a)
    c = [jnp.maximum(a[i], b[k - 1 - i]) if k - 1 - i < len(b) else a[i] for i in range(k)]
    return _merge_desc(c)


def _route_kernel(qp_ref, keys_ref, n1_ref, e1_ref, r2_ref, e2_ref, sc_ref, out_ref, *, pitch):
    k_top = PEER_TOPK
    nk = PEER_NKEYS
    nsub = qp_ref.shape[0] // LANES

    for side in range(2):
        kmat = keys_ref[side]
        for j in range(nsub):
            qj = qp_ref[j * LANES:(j + 1) * LANES, side * LANES:(side + 1) * LANES]
            sc_ref[side, pl.ds(j * pitch, nk), :] = lax.dot_general(
                kmat, qj, (((1,), (1,)), ((), ())), preferred_element_type=F32)

    def key_rows(kk):
        return pl.ds(kk, nsub, stride=pitch)

    def top_sorted(side):
        best = None
        for g in range(nk // k_top):
            grp = _sort_desc([sc_ref[side, key_rows(g * k_top + i), :] for i in range(k_top)])
            best = grp if best is None else _merge_top(best, grp)
        return best

    v1 = top_sorted(0)
    v2 = top_sorted(1)

    lens = [k_top // (a + 1) for a in range(k_top)]
    cand = [[v1[a] + v2[b] for b in range(lens[a])] for a in range(k_top)]
    top = cand[0]
    a = 1
    while lens[a] > 1:
        top = _merge_top(top, cand[a])
        a += 1
    top = _merge_top(top, [cand[i][0] for i in range(a, k_top)])
    thr = top[k_top - 1]

    m = cand[0][0]
    z = None
    for row in cand:
        for c in row:
            e = jnp.where(c >= thr, jnp.exp(c - m), 0.0)
            z = e if z is None else z + e
    inv_z = 1.0 / z

    inf = jnp.full(thr.shape, jnp.inf, F32)
    tb = []
    for b in range(k_top):
        t = inf
        for a in range(k_top):
            if b < lens[a]:
                t = jnp.minimum(t, jnp.where(cand[a][b] >= thr, v1[a], inf))
        tb.append(t)

    def bf16_bits(x):
        return pltpu.bitcast(x.astype(BF16).astype(F32), jnp.uint32)

    def pack2(lo, hi):
        return pltpu.bitcast((bf16_bits(lo) >> 16) | bf16_bits(hi), F32)

    for kk in range(nk):
        rows = key_rows(kk)
        s1 = sc_ref[0, rows, :]
        cnt = jnp.zeros_like(s1)
        for b in range(k_top):
            cnt = jnp.where(s1 >= tb[b], float(b + 1), cnt)
        out_ref[0, rows, :] = pack2(cnt, cnt)
        g1 = jnp.exp(s1 - v1[0]) * inv_z
        out_ref[1, rows, :] = pack2(g1, g1)

    def second(kk):
        s2 = sc_ref[1, key_rows(kk), :]
        rank = jnp.full(s2.shape, float(k_top), F32)
        for b in range(k_top - 1, -1, -1):
            rank = jnp.where(s2 >= v2[b], float(b), rank)
        return rank, jnp.exp(s2 - v2[0])

    for kp in range(nk // 2):
        r_lo, g_lo = second(2 * kp)
        r_hi, g_hi = second(2 * kp + 1)
        out_ref[2, key_rows(kp), :] = pack2(r_lo, r_hi)
        out_ref[3, key_rows(kp), :] = pack2(g_lo, g_hi)

    for j in range(nsub):
        n1_ref[j] = out_ref[0, pl.ds(j * pitch, nk), :]
        e1_ref[j] = out_ref[1, pl.ds(j * pitch, nk), :]
        r2_ref[j] = out_ref[2, pl.ds(j * pitch, nk // 2), :]
        e2_ref[j] = out_ref[3, pl.ds(j * pitch, nk // 2), :]


def _route(qp, keys):
    n = qp.shape[0]
    tt = min(SUBLANES * LANES, n)
    nsub = tt // LANES
    pitch = PEER_NKEYS + SUBLANES
    rows = nsub * pitch
    hk = 2 * LANES
    blk = lambda r: pl.BlockSpec((None, nsub, r, LANES), lambda i, h: (h, i, 0, 0))
    shp = lambda r: jax.ShapeDtypeStruct((PEER_HEADS, n // LANES, r, LANES), F32)
    nk, nk2 = PEER_NKEYS, PEER_NKEYS // 2
    return pl.pallas_call(
        functools.partial(_route_kernel, pitch=pitch),
        grid=(n // tt, PEER_HEADS),
        in_specs=[pl.BlockSpec((tt, hk), lambda i, h: (i, h)),
                  pl.BlockSpec((2, None, PEER_NKEYS, LANES), lambda i, h: (0, h, 0, 0))],
        out_specs=[blk(nk), blk(nk), blk(nk2), blk(nk2)],
        out_shape=[shp(nk), shp(nk), shp(nk2), shp(nk2)],
        scratch_shapes=[pltpu.VMEM((2, rows, LANES), F32), pltpu.VMEM((4, rows, LANES), F32)],
        compiler_params=_params("parallel", "parallel"),
        name="peer_route",
    )(qp, keys)


def _peer_kernel(xn_ref, x1_ref, n1_ref, e1_ref, r2_ref, e2_ref, u_ref, vt_ref, fn_ref,
                 y_ref, a_ref, hid_ref, acc_ref, *, tsub, tpc):
    e = pl.program_id(1)
    tt = xn_ref.shape[0]
    nsub = u_ref.shape[0] // tsub
    keys_per_sub = tsub // PEER_NKEYS
    nt = (((1,), (1,)), ((), ()))

    @pl.when(e == 0)
    def _():
        acc_ref[...] = jnp.zeros(acc_ref.shape, F32)

    half = PEER_NKEYS // 2
    zero = jnp.zeros((), BF16)

    def bcast_row(ref, h, j, row):
        w = jnp.broadcast_to(ref[h, j, pl.ds(row, 1), :], (SUBLANES, LANES))
        return pltpu.repeat(pltpu.bitcast(w, BF16), half // (2 * SUBLANES), axis=0)

    def keys_block(ref, h, j, hf):
        return pltpu.bitcast(ref[h, j, hf * half // 2:(hf + 1) * half // 2, :], BF16)

    npc = tt // tpc

    def scores(k, c):
        a_ref[k, :, c * tpc:(c + 1) * tpc] = lax.dot_general(
            u_ref[k * tsub:(k + 1) * tsub, :], xn_ref[c * tpc:(c + 1) * tpc, :], nt,
            preferred_element_type=F32)

    def gate(k, j):
        cols = slice(j * LANES, (j + 1) * LANES)
        row0 = (e * nsub + k) * keys_per_sub
        for hf in range(2):
            g = [None] * keys_per_sub
            for h in range(PEER_HEADS):
                r2 = keys_block(r2_ref, h, j, hf)
                e2 = keys_block(e2_ref, h, j, hf)
                for i1 in range(keys_per_sub):
                    sel = (jnp.where(r2 < bcast_row(n1_ref, h, j, row0 + i1), e2, zero)
                           * bcast_row(e1_ref, h, j, row0 + i1))
                    g[i1] = sel if g[i1] is None else g[i1] + sel
            for i1 in range(keys_per_sub):
                rows = slice(i1 * PEER_NKEYS + hf * half, i1 * PEER_NKEYS + (hf + 1) * half)
                hid_ref[k, rows, cols] = _gelu(a_ref[k, rows, cols].astype(BF16)) * g[i1]

    def values(k, c):
        cols = slice(c * tpc, (c + 1) * tpc)
        acc_ref[:, cols] += jnp.dot(vt_ref[:, k * tsub:(k + 1) * tsub], hid_ref[k, :, cols],
                                    preferred_element_type=F32)

    for c in range(npc):
        scores(0, c)
    for k in range(nsub + 1):
        pieces = []
        if k + 1 < nsub:
            pieces += [functools.partial(scores, k + 1, c) for c in range(npc)]
        if k >= 1:
            pieces += [functools.partial(values, k - 1, c) for c in range(npc)]
        slabs = [functools.partial(gate, k, j) for j in range(tt // LANES)] if k < nsub else []
        n_steps = max(len(pieces), len(slabs))
        for t in range(n_steps):
            for p in pieces[t * len(pieces) // n_steps:(t + 1) * len(pieces) // n_steps]:
                p()
            for s in slabs[t * len(slabs) // n_steps:(t + 1) * len(slabs) // n_steps]:
                s()

    @pl.when(e == pl.num_programs(1) - 1)
    def _():
        x2 = x1_ref[...] + acc_ref[...].T
        y_ref[...] = _rms(x2, fn_ref[...])


def _peer(xn, x1, n1, e1, r2, e2, w, tt=512, te=2048, tsub=512, tpc=256):
    n, d = xn.shape
    tt = min(tt, n)
    ne = w['expert_u'].shape[0]
    rt = lambda a: pl.BlockSpec((PEER_HEADS, tt // LANES) + a.shape[2:], lambda i, e: (0, i, 0, 0))
    return pl.pallas_call(
        functools.partial(_peer_kernel, tsub=tsub, tpc=min(tpc, tt)),
        grid=(n // tt, ne // te),
        in_specs=[pl.BlockSpec((tt, d), lambda i, e: (i, 0)),
                  pl.BlockSpec((tt, d), lambda i, e: (i, 0)),
                  rt(n1), rt(e1), rt(r2), rt(e2),
                  pl.BlockSpec((te, d), lambda i, e: (e, 0)),
                  pl.BlockSpec((d, te), lambda i, e: (0, e)),
                  pl.BlockSpec((1, d), lambda i, e: (0, 0))],
        out_specs=pl.BlockSpec((tt, d), lambda i, e: (i, 0)),
        out_shape=jax.ShapeDtypeStruct((n, d), F32),
        scratch_shapes=[pltpu.VMEM((te // tsub, tsub, tt), F32),
                        pltpu.VMEM((te // tsub, tsub, tt), BF16),
                        pltpu.VMEM((d, tt), F32)],
        compiler_params=_params("parallel", "arbitrary"),
        name="peer_experts",
    )(xn, x1, n1, e1, r2, e2, w['expert_u'], w['expert_vt'], w['final_norm'])


def _rot_cols(wm):
    hw = wm.shape[-1] // 2
    return jnp.concatenate([-wm[..., hw:], wm[..., :hw]], axis=-1)


def _s5_direction_params(lam_re, lam_im, log_dt, b_re, b_im, c_re, c_im):
    g, p, ch = SSM_GROUPS, SSM_STATE, SSM_GROUP_CH
    dt = jnp.exp(log_dt)[:, None]
    mag = jnp.exp(lam_re * dt)
    a_re = mag * jnp.cos(lam_im * dt)
    a_im = mag * jnp.sin(lam_im * dt)
    nr = a_re - 1.0
    den = lam_re * lam_re + lam_im * lam_im
    f_re = (nr * lam_re + a_im * lam_im) / den
    f_im = (a_im * lam_re - nr * lam_im) / den
    bb_re = f_re[..., None] * b_re - f_im[..., None] * b_im
    bb_im = f_re[..., None] * b_im + f_im[..., None] * b_re
    eye = jnp.eye(g, dtype=F32)
    per_blk = SSM_CHUNKS * LANES // SSM_WIDTH

    def expand(bb):
        m = jnp.einsum('gph,gk->ghkp', bb, eye).reshape(SSM_WIDTH // LANES, LANES, SSM_CHUNKS, LANES)
        return jnp.stack([m[c // per_blk, :, c, :] for c in range(SSM_CHUNKS)])

    def readout(cm):
        m = jnp.einsum('ghp,gk->gpkh', cm, eye).reshape(SSM_CHUNKS, LANES, SSM_WIDTH // LANES, LANES)
        return jnp.stack([m[c, :, c // per_blk, :] for c in range(SSM_CHUNKS)])

    wexp = jnp.concatenate([expand(bb_re), expand(bb_im)], axis=-1).astype(BF16)
    cmat = jnp.stack([readout(c_re), readout(-c_im)]).astype(BF16)
    a = jnp.concatenate([a_re.reshape(2, SUBLANES, LANES), a_im.reshape(2, SUBLANES, LANES)])
    return wexp, a, cmat


def _prepare(mix_norm, w_in, q_norm, kv_norm, w_uq, w_ukv, lam_re, lam_im, log_dt, b_re, b_im,
             c_re, c_im, d_skip, w_glu, b_glu, w_out, ffn_norm, w_query, sub_keys, expert_u,
             expert_v, final_norm):
    o1 = Q_LORA
    o2 = o1 + KV_LORA
    o3 = o2 + QK_ROPE
    wi = w_in[0]
    w_kpe = wi[:, o2:o3]
    w = {}
    w['mix_norm'] = mix_norm[0][None, :]
    w['w_in'] = jnp.concatenate([wi[:, :o2], w_kpe, _rot_cols(w_kpe), wi[:, o3:]], axis=1).astype(BF16)
    w['q_norm'] = q_norm[0][None, :]
    w['kv_norm'] = kv_norm[0][None, :]
    wq = w_uq[0]
    wq_pe = wq[..., QK_NOPE:]
    w['w_q'] = jnp.concatenate([wq[..., :QK_NOPE], wq_pe, _rot_cols(wq_pe)], axis=-1).reshape(
        Q_LORA, N_HEADS * QK_PAD).astype(BF16)
    w['w_kv'] = w_ukv[0].reshape(KV_LORA, N_HEADS * (QK_NOPE + V_HEAD)).astype(BF16)
    for name, d in (('f', 0), ('b', 1)):
        wexp, a, cmat = _s5_direction_params(lam_re[0, d], lam_im[0, d], log_dt[0, d], b_re[0, d],
                                             b_im[0, d], c_re[0, d], c_im[0, d])
        w['s5_w' + name], w['s5_a' + name], w['s5_c' + name] = wexp, a, cmat
    w['d_skip'] = d_skip[0].reshape(1, SSM_WIDTH)
    w['w_glu'] = w_glu[0].astype(BF16)
    w['b_glu'] = b_glu[0][None, :]
    mla_w = N_HEADS * V_HEAD
    w['w_out_att'] = w_out[0][:mla_w].astype(BF16)
    w['w_out_ssm'] = w_out[0][mla_w:].astype(BF16)
    w['ffn_norm'] = ffn_norm[0][None, :]
    w['w_query'] = w_query[0].astype(BF16)
    w['sub_keys'] = sub_keys[0].astype(BF16)
    w['expert_u'] = expert_u[0].astype(BF16)
    w['expert_vt'] = expert_v[0].astype(BF16).T
    w['final_norm'] = final_norm[None, :]
    return w


def _rope_table(s):
    inv = ROPE_THETA ** (-jnp.arange(0, QK_ROPE, 2, dtype=F32) / QK_ROPE)
    ang = jnp.arange(s, dtype=F32)[:, None] * inv[None, :]
    c, sn = jnp.cos(ang), jnp.sin(ang)
    return jnp.concatenate([c, c, sn, sn], axis=1)


def _encoder(x, w):
    bsz, s, d = x.shape
    q, k, v, u = _in_stage(x, _rope_table(s), w)
    att = _attention(q, k, v)
    yf, yb = _s5_scan(u, w)
    n = bsz * s
    x1, xn, qp = _mid_stage(x.reshape(n, d), att.reshape(n, -1), yf.reshape(n, -1),
                            yb.reshape(n, -1), u.reshape(n, -1), w)
    n1, e1, r2, e2 = _route(qp, w['sub_keys'])
    y = _peer(xn, x1, n1, e1, r2, e2, w)
    return y.reshape(bsz, s, d)


def kernel(x_prompt, x_sample, mix_norm, w_in, q_norm, kv_norm, w_uq, w_ukv, lam_re, lam_im, log_dt, b_re, b_im, c_re, c_im, d_skip, w_glu, b_glu, w_out, ffn_norm, w_query, sub_keys, expert_u, expert_v, final_norm):
    w = _prepare(mix_norm, w_in, q_norm, kv_norm, w_uq, w_ukv, lam_re, lam_im, log_dt, b_re, b_im,
                 c_re, c_im, d_skip, w_glu, b_glu, w_out, ffn_norm, w_query, sub_keys, expert_u,
                 expert_v, final_norm)
    return (_encoder(x_prompt, w), _encoder(x_sample, w))
```

```python
import functools
import math

import jax
import jax.numpy as jnp
from jax import lax
from jax.experimental import pallas as pl
from jax.experimental.pallas import tpu as pltpu

F32 = jnp.float32
BF16 = jnp.bfloat16

EPS = 1e-6
ROPE_THETA = 10000.0
LANES = 128
SUBLANES = 8
VMEM_LIMIT = 56 * 1024 * 1024

N_HEADS = 4
QK_NOPE = 128
QK_ROPE = 64
V_HEAD = 128
QK_PAD = 256
Q_LORA = 384
KV_LORA = 256
SSM_WIDTH = 512
SSM_GROUPS = 32
SSM_GROUP_CH = 16
SSM_STATE = 64
SSM_CHUNKS = SSM_GROUPS * SSM_STATE // LANES
PEER_HEADS = 8
PEER_NKEYS = 128
PEER_TOPK = 16
GELU_C = math.sqrt(2.0 / math.pi)


def _gelu(x):
    return 0.5 * x * (1.0 + jnp.tanh(GELU_C * (x + 0.044715 * (x * x * x))))


def _rms(x, g):
    return x * lax.rsqrt(jnp.mean(x * x, axis=-1, keepdims=True) + EPS) * g


def _params(*sem):
    return pltpu.CompilerParams(dimension_semantics=sem, vmem_limit_bytes=VMEM_LIMIT)


def _in_kernel(x_ref, cs_ref, g_ref, win_ref, qn_ref, kvn_ref, wq_ref, wkv_ref,
               q_ref, k_ref, v_ref, u_ref):
    x = x_ref[...]
    h = _rms(x, g_ref[...])
    proj = jnp.dot(h.astype(BF16), win_ref[...], preferred_element_type=F32)
    o1 = Q_LORA
    o2 = o1 + KV_LORA
    o3 = o2 + 2 * QK_ROPE
    cq = proj[:, :o1]
    ckv = proj[:, o1:o2]
    kp = proj[:, o2:o3]
    u_ref[...] = proj[:, o3:]
    cs = cs_ref[...]
    lane = lax.broadcasted_iota(jnp.int32, kp.shape, 1)
    t = kp * cs
    kpe = jnp.where(lane < QK_ROPE, t + pltpu.roll(t, QK_ROPE, axis=1), 0.0)
    scale = (QK_NOPE + QK_ROPE) ** -0.5
    q = jnp.dot(_rms(cq, qn_ref[...]).astype(BF16), wq_ref[...], preferred_element_type=F32) * scale
    kv = jnp.dot(_rms(ckv, kvn_ref[...]).astype(BF16), wkv_ref[...], preferred_element_type=F32)
    for hd in range(N_HEADS):
        b = hd * QK_PAD
        tq = q[:, b + QK_NOPE:b + QK_PAD] * cs
        q_ref[hd, :, :QK_NOPE] = q[:, b:b + QK_NOPE].astype(BF16)
        q_ref[hd, :, QK_NOPE:] = (tq + pltpu.roll(tq, QK_ROPE, axis=1)).astype(BF16)
        k_ref[hd, :, :QK_NOPE] = kv[:, b:b + QK_NOPE].astype(BF16)
        k_ref[hd, :, QK_NOPE:] = kpe.astype(BF16)
        v_ref[hd] = kv[:, b + QK_NOPE:b + QK_PAD].astype(BF16)


def _in_stage(x, cs, w, tt=512):
    bsz, s, d = x.shape
    tt = min(tt, s)
    grid = (bsz, s // tt)
    full = lambda a: pl.BlockSpec(a.shape, lambda b, i: (0,) * a.ndim)
    hs = lambda wd: pl.BlockSpec((None, N_HEADS, tt, wd), lambda b, i: (b, 0, i, 0))
    return pl.pallas_call(
        _in_kernel,
        grid=grid,
        in_specs=[pl.BlockSpec((None, tt, d), lambda b, i: (b, i, 0)),
                  pl.BlockSpec((tt, 2 * QK_ROPE), lambda b, i: (i, 0)),
                  full(w['mix_norm']), full(w['w_in']), full(w['q_norm']), full(w['kv_norm']),
                  full(w['w_q']), full(w['w_kv'])],
        out_specs=[hs(QK_PAD), hs(QK_PAD), hs(V_HEAD),
                   pl.BlockSpec((None, tt, SSM_WIDTH), lambda b, i: (b, i, 0))],
        out_shape=[jax.ShapeDtypeStruct((bsz, N_HEADS, s, QK_PAD), BF16),
                   jax.ShapeDtypeStruct((bsz, N_HEADS, s, QK_PAD), BF16),
                   jax.ShapeDtypeStruct((bsz, N_HEADS, s, V_HEAD), BF16),
                   jax.ShapeDtypeStruct((bsz, s, SSM_WIDTH), F32)],
        compiler_params=_params("parallel", "parallel"),
        name="in_stage",
    )(x, cs, w['mix_norm'], w['w_in'], w['q_norm'], w['kv_norm'], w['w_q'], w['w_kv'])


def _flash_kernel(q_ref, k_ref, v_ref, o_ref, m_ref, acc_ref, s0, s1, p0, p1, al0, al1, *, tk):
    n = k_ref.shape[0] // tk
    m_ref[...] = jnp.full(m_ref.shape, -jnp.inf, F32)
    acc_ref[...] = jnp.zeros(acc_ref.shape, F32)
    ones = jnp.ones((tk, LANES), BF16)
    s_buf, p_buf, al_buf = (s0, s1), (p0, p1), (al0, al1)

    def scores(j, slot):
        kj = k_ref[pl.ds(pl.multiple_of(j * tk, tk), tk), :]
        s_buf[slot][...] = lax.dot_general(q_ref[...], kj, (((1,), (1,)), ((), ())),
                                           preferred_element_type=F32)

    def exps(slot):
        s = s_buf[slot][...]
        m_prev = m_ref[...]
        m_new = jnp.maximum(m_prev, jnp.max(s, axis=1, keepdims=True))
        p_buf[slot][...] = jnp.exp(s - jnp.concatenate([m_new] * (tk // LANES), axis=1)).astype(BF16)
        al_buf[slot][...] = jnp.exp(m_prev - m_new)
        m_ref[...] = m_new

    def values(j, slot):
        vj = jnp.concatenate([v_ref[pl.ds(pl.multiple_of(j * tk, tk), tk), :], ones], axis=1)
        acc_ref[...] = (jnp.concatenate([al_buf[slot][...]] * 2, axis=1) * acc_ref[...]
                        + jnp.dot(p_buf[slot][...], vj, preferred_element_type=F32))

    scores(0, 0)
    scores(1, 1)
    exps(0)

    def body(jj, carry):
        j = 2 * jj + 2
        scores(j, 0)
        exps(1)
        values(j - 2, 0)
        scores(j + 1, 1)
        exps(0)
        values(j - 1, 1)
        return carry

    lax.fori_loop(0, (n - 2) // 2, body, 0)
    exps(1)
    values(n - 2, 0)
    values(n - 1, 1)
    o_ref[...] = (acc_ref[:, :V_HEAD] / acc_ref[:, V_HEAD:]).astype(o_ref.dtype)


def _attention(q, k, v, tq=1024, tk=512):
    bsz, nh, s, _ = q.shape
    tq = min(tq, s)
    tk = min(tk, s // 2)
    assert s % (2 * tk) == 0 and s % tq == 0
    return pl.pallas_call(
        functools.partial(_flash_kernel, tk=tk),
        grid=(bsz, nh, s // tq),
        in_specs=[pl.BlockSpec((None, None, tq, QK_PAD), lambda b, h, i: (b, h, i, 0)),
                  pl.BlockSpec((None, None, s, QK_PAD), lambda b, h, i: (b, h, 0, 0)),
                  pl.BlockSpec((None, None, s, V_HEAD), lambda b, h, i: (b, h, 0, 0))],
        out_specs=pl.BlockSpec((None, tq, V_HEAD), lambda b, h, i: (b, i, h)),
        out_shape=jax.ShapeDtypeStruct((bsz, s, nh * V_HEAD), BF16),
        scratch_shapes=[pltpu.VMEM((tq, LANES), F32), pltpu.VMEM((tq, 2 * V_HEAD), F32),
                        pltpu.VMEM((tq, tk), F32), pltpu.VMEM((tq, tk), F32),
                        pltpu.VMEM((tq, tk), BF16), pltpu.VMEM((tq, tk), BF16),
                        pltpu.VMEM((tq, LANES), F32), pltpu.VMEM((tq, LANES), F32)],
        compiler_params=_params("parallel", "parallel", "arbitrary"),
        name="attention",
    )(q, k, v)


def _s5_kernel(uf_ref, ub_ref, wf_ref, wb_ref, af_ref, ab_ref, cf_ref, cb_ref,
               yf_ref, yb_ref, fre, fim, bre, bim, sfre, sfim, sbre, sbim, st_ref, *, tc, pitch):
    half = SSM_CHUNKS // 2 * pitch

    @pl.when(pl.program_id(1) == 0)
    def _():
        st_ref[...] = jnp.zeros(st_ref.shape, F32)

    for u_ref, w_ref, re, im in ((uf_ref, wf_ref, fre, fim), (ub_ref, wb_ref, bre, bim)):
        for blk in range(SSM_WIDTH // LANES):
            ub = u_ref[:, blk * LANES:(blk + 1) * LANES].astype(BF16)
            for cc in range(SSM_CHUNKS * LANES // SSM_WIDTH):
                ch = blk * (SSM_CHUNKS * LANES // SSM_WIDTH) + cc
                r = jnp.dot(ub, w_ref[ch], preferred_element_type=F32)
                re[pl.ds(ch * pitch, tc), :] = r[:, :LANES]
                im[pl.ds(ch * pitch, tc), :] = r[:, LANES:]

    af = af_ref[...]
    ab = ab_ref[...]

    def step(a, st, re, im, sre, sim, t):
        out = []
        for hf in range(2):
            rows = pl.ds(hf * half + t, SUBLANES, stride=pitch)
            a_re, a_im = a[hf], a[2 + hf]
            s_re, s_im = st[hf], st[2 + hf]
            n_re = a_re * s_re - a_im * s_im + re[rows, :]
            n_im = a_re * s_im + a_im * s_re + im[rows, :]
            sre[rows, :] = n_re
            sim[rows, :] = n_im
            out.append((n_re, n_im))
        return (out[0][0], out[1][0], out[0][1], out[1][1])

    def body(t, carry):
        sf, sb = carry
        sf = step(af, sf, fre, fim, sfre, sfim, t)
        sb = step(ab, sb, bre, bim, sbre, sbim, tc - 1 - t)
        return (sf, sb)

    st = st_ref[...]
    init = (tuple(st[0, i] for i in range(4)), tuple(st[1, i] for i in range(4)))
    sf, sb = lax.fori_loop(0, tc, body, init, unroll=8)
    for i in range(4):
        st_ref[0, i] = sf[i]
        st_ref[1, i] = sb[i]

    for y_ref, c_ref, re, im in ((yf_ref, cf_ref, sfre, sfim), (yb_ref, cb_ref, sbre, sbim)):
        for blk in range(SSM_WIDTH // LANES):
            acc = None
            for cc in range(SSM_CHUNKS * LANES // SSM_WIDTH):
                ch = blk * (SSM_CHUNKS * LANES // SSM_WIDTH) + cc
                s = jnp.concatenate([re[pl.ds(ch * pitch, tc), :].astype(BF16),
                                     im[pl.ds(ch * pitch, tc), :].astype(BF16)], axis=1)
                d = jnp.dot(s, c_ref[ch], preferred_element_type=F32)
                acc = d if acc is None else acc + d
            y_ref[:, blk * LANES:(blk + 1) * LANES] = acc


def _s5_scan(u, w, tc=512):
    bsz, s, _ = u.shape
    tc = min(tc, s)
    n = s // tc
    pitch = tc + SUBLANES // 2
    rows = SSM_CHUNKS * pitch
    full = lambda a: pl.BlockSpec(a.shape, lambda b, c: (0,) * a.ndim)
    ublk = lambda f: pl.BlockSpec((None, tc, SSM_WIDTH), f)
    fwd = lambda b, c: (b, c, 0)
    bwd = lambda b, c: (b, n - 1 - c, 0)
    return pl.pallas_call(
        functools.partial(_s5_kernel, tc=tc, pitch=pitch),
        grid=(bsz, n),
        in_specs=[ublk(fwd), ublk(bwd), full(w['s5_wf']), full(w['s5_wb']),
                  full(w['s5_af']), full(w['s5_ab']), full(w['s5_cf']), full(w['s5_cb'])],
        out_specs=[ublk(fwd), ublk(bwd)],
        out_shape=[jax.ShapeDtypeStruct(u.shape, F32), jax.ShapeDtypeStruct(u.shape, F32)],
        scratch_shapes=[pltpu.VMEM((rows, LANES), F32) for _ in range(8)]
                       + [pltpu.VMEM((2, 4, SUBLANES, LANES), F32)],
        compiler_params=_params("parallel", "arbitrary"),
        name="s5_scan",
    )(u, u, w['s5_wf'], w['s5_wb'], w['s5_af'], w['s5_ab'], w['s5_cf'], w['s5_cb'])


def _mid_kernel(x_ref, att_ref, yf_ref, yb_ref, u_ref, dsk_ref, wglu_ref, bglu_ref,
                woa_ref, wos_ref, fn_ref, wqry_ref, x1_ref, xn_ref, qp_ref):
    y = dsk_ref[...] * u_ref[...] + yf_ref[...] + yb_ref[...]
    z = _gelu(y)
    gate = jnp.dot(z.astype(BF16), wglu_ref[...], preferred_element_type=F32) + bglu_ref[...]
    z = z * (1.0 / (1.0 + jnp.exp(-gate)))
    mix = (jnp.dot(att_ref[...], woa_ref[...], preferred_element_type=F32)
           + jnp.dot(z.astype(BF16), wos_ref[...], preferred_element_type=F32))
    x1 = x_ref[...] + mix
    x1_ref[...] = x1
    xn = _rms(x1, fn_ref[...]).astype(BF16)
    xn_ref[...] = xn
    qp_ref[...] = jnp.dot(xn, wqry_ref[...], preferred_element_type=F32).astype(BF16)


def _mid_stage(x, att, yf, yb, u, w, tt=512):
    n, d = x.shape
    tt = min(tt, n)
    full = lambda a: pl.BlockSpec(a.shape, lambda i: (0,) * a.ndim)
    row = lambda wd: pl.BlockSpec((tt, wd), lambda i: (i, 0))
    nq = w['w_query'].shape[1]
    return pl.pallas_call(
        _mid_kernel,
        grid=(n // tt,),
        in_specs=[row(d), row(att.shape[1]), row(SSM_WIDTH), row(SSM_WIDTH), row(SSM_WIDTH),
                  full(w['d_skip']), full(w['w_glu']), full(w['b_glu']), full(w['w_out_att']),
                  full(w['w_out_ssm']), full(w['ffn_norm']), full(w['w_query'])],
        out_specs=[row(d), row(d), row(nq)],
        out_shape=[jax.ShapeDtypeStruct((n, d), F32), jax.ShapeDtypeStruct((n, d), BF16),
                   jax.ShapeDtypeStruct((n, nq), BF16)],
        compiler_params=_params("parallel"),
        name="mid_stage",
    )(x, att, yf, yb, u, w['d_skip'], w['w_glu'], w['b_glu'], w['w_out_att'], w['w_out_ssm'],
      w['ffn_norm'], w['w_query'])


def _merge_desc(x):
    n = len(x)
    if n == 1:
        return x
    h = n // 2
    hi = [jnp.maximum(x[i], x[i + h]) for i in range(h)]
    lo = [jnp.minimum(x[i], x[i + h]) for i in range(h)]
    return _merge_desc(hi) + _merge_desc(lo)


def _sort_desc(x):
    n = len(x)
    if n == 1:
        return x
    h = n // 2
    return _merge_desc(_sort_desc(x[:h]) + _sort_desc(x[h:])[::-1])


def _merge_top(a, b):
    k = len(a)
    c = [jnp.maximum(a[i], b[k - 1 - i]) if k - 1 - i < len(b) else a[i] for i in range(k)]
    return _merge_desc(c)


def _route_kernel(qp_ref, keys_ref, n1_ref, e1_ref, r2_ref, e2_ref, sc_ref, out_ref, *, pitch):
    k_top = PEER_TOPK
    nk = PEER_NKEYS
    nsub = qp_ref.shape[0] // LANES

    for side in range(2):
        kmat = keys_ref[side]
        for j in range(nsub):
            qj = qp_ref[j * LANES:(j + 1) * LANES, side * LANES:(side + 1) * LANES]
            sc_ref[side, pl.ds(j * pitch, nk), :] = lax.dot_general(
                kmat, qj, (((1,), (1,)), ((), ())), preferred_element_type=F32)

    def key_rows(kk):
        return pl.ds(kk, nsub, stride=pitch)

    def top_sorted(side):
        best = None
        for g in range(nk // k_top):
            grp = _sort_desc([sc_ref[side, key_rows(g * k_top + i), :] for i in range(k_top)])
            best = grp if best is None else _merge_top(best, grp)
        return best

    v1 = top_sorted(0)
    v2 = top_sorted(1)

    lens = [k_top // (a + 1) for a in range(k_top)]
    cand = [[v1[a] + v2[b] for b in range(lens[a])] for a in range(k_top)]
    top = cand[0]
    a = 1
    while lens[a] > 1:
        top = _merge_top(top, cand[a])
        a += 1
    top = _merge_top(top, [cand[i][0] for i in range(a, k_top)])
    thr = top[k_top - 1]

    m = cand[0][0]
    z = None
    for row in cand:
        for c in row:
            e = jnp.where(c >= thr, jnp.exp(c - m), 0.0)
            z = e if z is None else z + e
    inv_z = 1.0 / z

    inf = jnp.full(thr.shape, jnp.inf, F32)
    tb = []
    for b in range(k_top):
        t = inf
        for a in range(k_top):
            if b < lens[a]:
                t = jnp.minimum(t, jnp.where(cand[a][b] >= thr, v1[a], inf))
        tb.append(t)

    for kk in range(nk):
        rows = key_rows(kk)
        s1 = sc_ref[0, rows, :]
        cnt = jnp.zeros_like(s1)
        for b in range(k_top):
            cnt = jnp.where(s1 >= tb[b], float(b + 1), cnt)
        out_ref[0, rows, :] = cnt
        out_ref[1, rows, :] = jnp.exp(s1 - v1[0]) * inv_z
        s2 = sc_ref[1, rows, :]
        rank = jnp.full(s2.shape, float(k_top), F32)
        for b in range(k_top - 1, -1, -1):
            rank = jnp.where(s2 >= v2[b], float(b), rank)
        out_ref[2, rows, :] = rank
        out_ref[3, rows, :] = jnp.exp(s2 - v2[0])

    for j in range(nsub):
        cols = slice(j * LANES, (j + 1) * LANES)
        rows = pl.ds(j * pitch, nk)
        n1_ref[:, cols] = out_ref[0, rows, :]
        e1_ref[:, cols] = out_ref[1, rows, :]
        r2_ref[:, cols] = out_ref[2, rows, :].astype(BF16)
        e2_ref[:, cols] = out_ref[3, rows, :].astype(BF16)


def _route(qp, keys):
    n = qp.shape[0]
    tt = min(SUBLANES * LANES, n)
    nsub = tt // LANES
    pitch = PEER_NKEYS + SUBLANES
    rows = nsub * pitch
    hk = 2 * LANES
    out_blk = pl.BlockSpec((None, PEER_NKEYS, tt), lambda i, h: (h, 0, i))
    shp = lambda dt: jax.ShapeDtypeStruct((PEER_HEADS, PEER_NKEYS, n), dt)
    return pl.pallas_call(
        functools.partial(_route_kernel, pitch=pitch),
        grid=(n // tt, PEER_HEADS),
        in_specs=[pl.BlockSpec((tt, hk), lambda i, h: (i, h)),
                  pl.BlockSpec((2, None, PEER_NKEYS, LANES), lambda i, h: (0, h, 0, 0))],
        out_specs=[out_blk, out_blk, out_blk, out_blk],
        out_shape=[shp(F32), shp(F32), shp(BF16), shp(BF16)],
        scratch_shapes=[pltpu.VMEM((2, rows, LANES), F32), pltpu.VMEM((4, rows, LANES), F32)],
        compiler_params=_params("parallel", "parallel"),
        name="peer_route",
    )(qp, keys)


def _peer_kernel(xn_ref, x1_ref, n1_ref, e1_ref, r2_ref, e2_ref, u_ref, vt_ref, fn_ref,
                 y_ref, a_ref, hid_ref, acc_ref, *, tcol):
    e = pl.program_id(1)
    tt = xn_ref.shape[0]
    rows_per_tile = u_ref.shape[0] // PEER_NKEYS

    @pl.when(e == 0)
    def _():
        acc_ref[...] = jnp.zeros(acc_ref.shape, F32)

    a_ref[...] = lax.dot_general(u_ref[...], xn_ref[...], (((1,), (1,)), ((), ())),
                                 preferred_element_type=F32)

    for i1 in range(rows_per_tile):
        row = e * rows_per_tile + i1
        for c in range(tt // tcol):
            cols = slice(c * tcol, (c + 1) * tcol)
            g = None
            for h in range(PEER_HEADS):
                n1 = jnp.broadcast_to(n1_ref[h, pl.ds(row, 1), cols].astype(BF16), (PEER_NKEYS, tcol))
                e1 = jnp.broadcast_to(e1_ref[h, pl.ds(row, 1), cols].astype(BF16), (PEER_NKEYS, tcol))
                sel = jnp.where(r2_ref[h, :, cols] < n1, e2_ref[h, :, cols], jnp.zeros((), BF16)) * e1
                g = sel if g is None else g + sel
            a = a_ref[i1 * PEER_NKEYS:(i1 + 1) * PEER_NKEYS, cols].astype(BF16)
            hid_ref[i1 * PEER_NKEYS:(i1 + 1) * PEER_NKEYS, cols] = _gelu(a) * g

    acc_ref[...] += jnp.dot(vt_ref[...], hid_ref[...], preferred_element_type=F32)

    @pl.when(e == pl.num_programs(1) - 1)
    def _():
        x2 = x1_ref[...] + acc_ref[...].T
        y_ref[...] = _rms(x2, fn_ref[...])


def _peer(xn, x1, n1, e1, r2, e2, w, tt=512, te=1024, tcol=256):
    n, d = xn.shape
    tt = min(tt, n)
    tcol = min(tcol, tt)
    ne = w['expert_u'].shape[0]
    rt = lambda: pl.BlockSpec((PEER_HEADS, PEER_NKEYS, tt), lambda i, e: (0, 0, i))
    return pl.pallas_call(
        functools.partial(_peer_kernel, tcol=tcol),
        grid=(n // tt, ne // te),
        in_specs=[pl.BlockSpec((tt, d), lambda i, e: (i, 0)),
                  pl.BlockSpec((tt, d), lambda i, e: (i, 0)),
                  rt(), rt(), rt(), rt(),
                  pl.BlockSpec((te, d), lambda i, e: (e, 0)),
                  pl.BlockSpec((d, te), lambda i, e: (0, e)),
                  pl.BlockSpec((1, d), lambda i, e: (0, 0))],
        out_specs=pl.BlockSpec((tt, d), lambda i, e: (i, 0)),
        out_shape=jax.ShapeDtypeStruct((n, d), F32),
        scratch_shapes=[pltpu.VMEM((te, tt), F32), pltpu.VMEM((te, tt), BF16),
                        pltpu.VMEM((d, tt), F32)],
        compiler_params=_params("parallel", "arbitrary"),
        name="peer_experts",
    )(xn, x1, n1, e1, r2, e2, w['expert_u'], w['expert_vt'], w['final_norm'])


def _rot_cols(wm):
    hw = wm.shape[-1] // 2
    return jnp.concatenate([-wm[..., hw:], wm[..., :hw]], axis=-1)


def _s5_direction_params(lam_re, lam_im, log_dt, b_re, b_im, c_re, c_im):
    g, p, ch = SSM_GROUPS, SSM_STATE, SSM_GROUP_CH
    dt = jnp.exp(log_dt)[:, None]
    mag = jnp.exp(lam_re * dt)
    a_re = mag * jnp.cos(lam_im * dt)
    a_im = mag * jnp.sin(lam_im * dt)
    nr = a_re - 1.0
    den = lam_re * lam_re + lam_im * lam_im
    f_re = (nr * lam_re + a_im * lam_im) / den
    f_im = (a_im * lam_re - nr * lam_im) / den
    bb_re = f_re[..., None] * b_re - f_im[..., None] * b_im
    bb_im = f_re[..., None] * b_im + f_im[..., None] * b_re
    eye = jnp.eye(g, dtype=F32)
    per_blk = SSM_CHUNKS * LANES // SSM_WIDTH

    def expand(bb):
        m = jnp.einsum('gph,gk->ghkp', bb, eye).reshape(SSM_WIDTH // LANES, LANES, SSM_CHUNKS, LANES)
        return jnp.stack([m[c // per_blk, :, c, :] for c in range(SSM_CHUNKS)])

    def readout(cm):
        m = jnp.einsum('ghp,gk->gpkh', cm, eye).reshape(SSM_CHUNKS, LANES, SSM_WIDTH // LANES, LANES)
        return jnp.stack([m[c, :, c // per_blk, :] for c in range(SSM_CHUNKS)])

    wexp = jnp.concatenate([expand(bb_re), expand(bb_im)], axis=-1).astype(BF16)
    cmat = jnp.concatenate([readout(c_re), readout(-c_im)], axis=1).astype(BF16)
    a = jnp.concatenate([a_re.reshape(2, SUBLANES, LANES), a_im.reshape(2, SUBLANES, LANES)])
    return wexp, a, cmat


def _prepare(mix_norm, w_in, q_norm, kv_norm, w_uq, w_ukv, lam_re, lam_im, log_dt, b_re, b_im,
             c_re, c_im, d_skip, w_glu, b_glu, w_out, ffn_norm, w_query, sub_keys, expert_u,
             expert_v, final_norm):
    o1 = Q_LORA
    o2 = o1 + KV_LORA
    o3 = o2 + QK_ROPE
    wi = w_in[0]
    w_kpe = wi[:, o2:o3]
    w = {}
    w['mix_norm'] = mix_norm[0][None, :]
    w['w_in'] = jnp.concatenate([wi[:, :o2], w_kpe, _rot_cols(w_kpe), wi[:, o3:]], axis=1).astype(BF16)
    w['q_norm'] = q_norm[0][None, :]
    w['kv_norm'] = kv_norm[0][None, :]
    wq = w_uq[0]
    wq_pe = wq[..., QK_NOPE:]
    w['w_q'] = jnp.concatenate([wq[..., :QK_NOPE], wq_pe, _rot_cols(wq_pe)], axis=-1).reshape(
        Q_LORA, N_HEADS * QK_PAD).astype(BF16)
    w['w_kv'] = w_ukv[0].reshape(KV_LORA, N_HEADS * (QK_NOPE + V_HEAD)).astype(BF16)
    for name, d in (('f', 0), ('b', 1)):
        wexp, a, cmat = _s5_direction_params(lam_re[0, d], lam_im[0, d], log_dt[0, d], b_re[0, d],
                                             b_im[0, d], c_re[0, d], c_im[0, d])
        w['s5_w' + name], w['s5_a' + name], w['s5_c' + name] = wexp, a, cmat
    w['d_skip'] = d_skip[0].reshape(1, SSM_WIDTH)
    w['w_glu'] = w_glu[0].astype(BF16)
    w['b_glu'] = b_glu[0][None, :]
    mla_w = N_HEADS * V_HEAD
    w['w_out_att'] = w_out[0][:mla_w].astype(BF16)
    w['w_out_ssm'] = w_out[0][mla_w:].astype(BF16)
    w['ffn_norm'] = ffn_norm[0][None, :]
    w['w_query'] = w_query[0].astype(BF16)
    w['sub_keys'] = sub_keys[0].astype(BF16)
    w['expert_u'] = expert_u[0].astype(BF16)
    w['expert_vt'] = expert_v[0].astype(BF16).T
    w['final_norm'] = final_norm[None, :]
    return w


def _rope_table(s):
    inv = ROPE_THETA ** (-jnp.arange(0, QK_ROPE, 2, dtype=F32) / QK_ROPE)
    ang = jnp.arange(s, dtype=F32)[:, None] * inv[None, :]
    c, sn = jnp.cos(ang), jnp.sin(ang)
    return jnp.concatenate([c, c, sn, sn], axis=1)


def _encoder(x, w):
    bsz, s, d = x.shape
    q, k, v, u = _in_stage(x, _rope_table(s), w)
    att = _attention(q, k, v)
    yf, yb = _s5_scan(u, w)
    n = bsz * s
    x1, xn, qp = _mid_stage(x.reshape(n, d), att.reshape(n, -1), yf.reshape(n, -1),
                            yb.reshape(n, -1), u.reshape(n, -1), w)
    n1, e1, r2, e2 = _route(qp, w['sub_keys'])
    y = _peer(xn, x1, n1, e1, r2, e2, w)
    return y.reshape(bsz, s, d)


def kernel(x_prompt, x_sample, mix_norm, w_in, q_norm, kv_norm, w_uq, w_ukv, lam_re, lam_im, log_dt, b_re, b_im, c_re, c_im, d_skip, w_glu, b_glu, w_out, ffn_norm, w_query, sub_keys, expert_u, expert_v, final_norm):
    w = _prepare(mix_norm, w_in, q_norm, kv_norm, w_uq, w_ukv, lam_re, lam_im, log_dt, b_re, b_im,
                 c_re, c_im, d_skip, w_glu, b_glu, w_out, ffn_norm, w_query, sub_keys, expert_u,
                 expert_v, final_norm)
    return (_encoder(x_prompt, w), _encoder(x_sample, w))
```

```python
import functools
import math

import jax
import jax.numpy as jnp
from jax import lax
from jax.experimental import pallas as pl
from jax.experimental.pallas import tpu as pltpu

F32 = jnp.float32
BF16 = jnp.bfloat16

EPS = 1e-6
ROPE_THETA = 10000.0
LANES = 128
SUBLANES = 8
VMEM_LIMIT = 56 * 1024 * 1024

N_HEADS = 4
QK_NOPE = 128
QK_ROPE = 64
V_HEAD = 128
QK_PAD = 256
Q_LORA = 384
KV_LORA = 256
SSM_WIDTH = 512
SSM_GROUPS = 32
SSM_GROUP_CH = 16
SSM_STATE = 64
SSM_CHUNKS = SSM_GROUPS * SSM_STATE // LANES
PEER_HEADS = 8
PEER_NKEYS = 128
PEER_TOPK = 16
GELU_C = math.sqrt(2.0 / math.pi)


def _gelu(x):
    return 0.5 * x * (1.0 + jnp.tanh(GELU_C * (x + 0.044715 * (x * x * x))))


def _rms(x, g):
    return x * lax.rsqrt(jnp.mean(x * x, axis=-1, keepdims=True) + EPS) * g


def _params(*sem):
    return pltpu.CompilerParams(dimension_semantics=sem, vmem_limit_bytes=VMEM_LIMIT)


def _in_kernel(x_ref, cs_ref, g_ref, win_ref, qn_ref, kvn_ref, wq_ref, wkv_ref,
               q_ref, k_ref, v_ref, u_ref):
    x = x_ref[...]
    h = _rms(x, g_ref[...])
    proj = jnp.dot(h.astype(BF16), win_ref[...], preferred_element_type=F32)
    o1 = Q_LORA
    o2 = o1 + KV_LORA
    o3 = o2 + 2 * QK_ROPE
    cq = proj[:, :o1]
    ckv = proj[:, o1:o2]
    kp = proj[:, o2:o3]
    u_ref[...] = proj[:, o3:]
    cs = cs_ref[...]
    lane = lax.broadcasted_iota(jnp.int32, kp.shape, 1)
    t = kp * cs
    kpe = jnp.where(lane < QK_ROPE, t + pltpu.roll(t, QK_ROPE, axis=1), 0.0)
    scale = (QK_NOPE + QK_ROPE) ** -0.5
    q = jnp.dot(_rms(cq, qn_ref[...]).astype(BF16), wq_ref[...], preferred_element_type=F32) * scale
    kv = jnp.dot(_rms(ckv, kvn_ref[...]).astype(BF16), wkv_ref[...], preferred_element_type=F32)
    for hd in range(N_HEADS):
        b = hd * QK_PAD
        tq = q[:, b + QK_NOPE:b + QK_PAD] * cs
        q_ref[hd, :, :QK_NOPE] = q[:, b:b + QK_NOPE].astype(BF16)
        q_ref[hd, :, QK_NOPE:] = (tq + pltpu.roll(tq, QK_ROPE, axis=1)).astype(BF16)
        k_ref[hd, :, :QK_NOPE] = kv[:, b:b + QK_NOPE].astype(BF16)
        k_ref[hd, :, QK_NOPE:] = kpe.astype(BF16)
        v_ref[hd] = kv[:, b + QK_NOPE:b + QK_PAD].astype(BF16)


def _in_stage(x, cs, w, tt=512):
    bsz, s, d = x.shape
    tt = min(tt, s)
    grid = (bsz, s // tt)
    full = lambda a: pl.BlockSpec(a.shape, lambda b, i: (0,) * a.ndim)
    hs = lambda wd: pl.BlockSpec((None, N_HEADS, tt, wd), lambda b, i: (b, 0, i, 0))
    return pl.pallas_call(
        _in_kernel,
        grid=grid,
        in_specs=[pl.BlockSpec((None, tt, d), lambda b, i: (b, i, 0)),
                  pl.BlockSpec((tt, 2 * QK_ROPE), lambda b, i: (i, 0)),
                  full(w['mix_norm']), full(w['w_in']), full(w['q_norm']), full(w['kv_norm']),
                  full(w['w_q']), full(w['w_kv'])],
        out_specs=[hs(QK_PAD), hs(QK_PAD), hs(V_HEAD),
                   pl.BlockSpec((None, tt, SSM_WIDTH), lambda b, i: (b, i, 0))],
        out_shape=[jax.ShapeDtypeStruct((bsz, N_HEADS, s, QK_PAD), BF16),
                   jax.ShapeDtypeStruct((bsz, N_HEADS, s, QK_PAD), BF16),
                   jax.ShapeDtypeStruct((bsz, N_HEADS, s, V_HEAD), BF16),
                   jax.ShapeDtypeStruct((bsz, s, SSM_WIDTH), F32)],
        compiler_params=_params("parallel", "parallel"),
        name="in_stage",
    )(x, cs, w['mix_norm'], w['w_in'], w['q_norm'], w['kv_norm'], w['w_q'], w['w_kv'])


def _flash_kernel(q_ref, k_ref, v_ref, o_ref, m_ref, acc_ref, s0, s1, p0, p1, al0, al1, *, tk):
    n = k_ref.shape[0] // tk
    m_ref[...] = jnp.full(m_ref.shape, -jnp.inf, F32)
    acc_ref[...] = jnp.zeros(acc_ref.shape, F32)
    ones = jnp.ones((tk, LANES), BF16)
    s_buf, p_buf, al_buf = (s0, s1), (p0, p1), (al0, al1)

    def scores(j, slot):
        kj = k_ref[pl.ds(pl.multiple_of(j * tk, tk), tk), :]
        s_buf[slot][...] = lax.dot_general(q_ref[...], kj, (((1,), (1,)), ((), ())),
                                           preferred_element_type=F32)

    def exps(slot):
        s = s_buf[slot][...]
        m_prev = m_ref[...]
        m_new = jnp.maximum(m_prev, jnp.max(s, axis=1, keepdims=True))
        p_buf[slot][...] = jnp.exp(s - jnp.concatenate([m_new] * (tk // LANES), axis=1)).astype(BF16)
        al_buf[slot][...] = jnp.exp(m_prev - m_new)
        m_ref[...] = m_new

    def values(j, slot):
        vj = jnp.concatenate([v_ref[pl.ds(pl.multiple_of(j * tk, tk), tk), :], ones], axis=1)
        acc_ref[...] = (jnp.concatenate([al_buf[slot][...]] * 2, axis=1) * acc_ref[...]
                        + jnp.dot(p_buf[slot][...], vj, preferred_element_type=F32))

    scores(0, 0)
    scores(1, 1)
    exps(0)

    def body(jj, carry):
        j = 2 * jj + 2
        scores(j, 0)
        exps(1)
        values(j - 2, 0)
        scores(j + 1, 1)
        exps(0)
        values(j - 1, 1)
        return carry

    lax.fori_loop(0, (n - 2) // 2, body, 0)
    exps(1)
    values(n - 2, 0)
    values(n - 1, 1)
    o_ref[...] = (acc_ref[:, :V_HEAD] / acc_ref[:, V_HEAD:]).astype(o_ref.dtype)


def _attention(q, k, v, tq=1024, tk=1024):
    bsz, nh, s, _ = q.shape
    tq = min(tq, s)
    tk = min(tk, s // 2)
    assert s % (2 * tk) == 0 and s % tq == 0
    return pl.pallas_call(
        functools.partial(_flash_kernel, tk=tk),
        grid=(bsz, nh, s // tq),
        in_specs=[pl.BlockSpec((None, None, tq, QK_PAD), lambda b, h, i: (b, h, i, 0)),
                  pl.BlockSpec((None, None, s, QK_PAD), lambda b, h, i: (b, h, 0, 0)),
                  pl.BlockSpec((None, None, s, V_HEAD), lambda b, h, i: (b, h, 0, 0))],
        out_specs=pl.BlockSpec((None, tq, V_HEAD), lambda b, h, i: (b, i, h)),
        out_shape=jax.ShapeDtypeStruct((bsz, s, nh * V_HEAD), BF16),
        scratch_shapes=[pltpu.VMEM((tq, LANES), F32), pltpu.VMEM((tq, 2 * V_HEAD), F32),
                        pltpu.VMEM((tq, tk), F32), pltpu.VMEM((tq, tk), F32),
                        pltpu.VMEM((tq, tk), BF16), pltpu.VMEM((tq, tk), BF16),
                        pltpu.VMEM((tq, LANES), F32), pltpu.VMEM((tq, LANES), F32)],
        compiler_params=_params("parallel", "parallel", "arbitrary"),
        name="attention",
    )(q, k, v)


def _s5_kernel(uf_ref, ub_ref, wf_ref, wb_ref, af_ref, ab_ref, cf_ref, cb_ref,
               yf_ref, yb_ref, fre, fim, bre, bim, sfre, sfim, sbre, sbim, st_ref, *, tc, pitch):
    half = SSM_CHUNKS // 2 * pitch

    @pl.when(pl.program_id(1) == 0)
    def _():
        st_ref[...] = jnp.zeros(st_ref.shape, F32)

    for u_ref, w_ref, re, im in ((uf_ref, wf_ref, fre, fim), (ub_ref, wb_ref, bre, bim)):
        for blk in range(SSM_WIDTH // LANES):
            ub = u_ref[:, blk * LANES:(blk + 1) * LANES].astype(BF16)
            for cc in range(SSM_CHUNKS * LANES // SSM_WIDTH):
                ch = blk * (SSM_CHUNKS * LANES // SSM_WIDTH) + cc
                r = jnp.dot(ub, w_ref[ch], preferred_element_type=F32)
                re[pl.ds(ch * pitch, tc), :] = r[:, :LANES]
                im[pl.ds(ch * pitch, tc), :] = r[:, LANES:]

    af = af_ref[...]
    ab = ab_ref[...]

    def step(a, st, re, im, sre, sim, t):
        out = []
        for hf in range(2):
            rows = pl.ds(hf * half + t, SUBLANES, stride=pitch)
            a_re, a_im = a[hf], a[2 + hf]
            s_re, s_im = st[hf], st[2 + hf]
            n_re = a_re * s_re - a_im * s_im + re[rows, :]
            n_im = a_re * s_im + a_im * s_re + im[rows, :]
            sre[rows, :] = n_re
            sim[rows, :] = n_im
            out.append((n_re, n_im))
        return (out[0][0], out[1][0], out[0][1], out[1][1])

    def body(t, carry):
        sf, sb = carry
        sf = step(af, sf, fre, fim, sfre, sfim, t)
        sb = step(ab, sb, bre, bim, sbre, sbim, tc - 1 - t)
        return (sf, sb)

    st = st_ref[...]
    init = (tuple(st[0, i] for i in range(4)), tuple(st[1, i] for i in range(4)))
    sf, sb = lax.fori_loop(0, tc, body, init, unroll=8)
    for i in range(4):
        st_ref[0, i] = sf[i]
        st_ref[1, i] = sb[i]

    for y_ref, c_ref, re, im in ((yf_ref, cf_ref, sfre, sfim), (yb_ref, cb_ref, sbre, sbim)):
        for blk in range(SSM_WIDTH // LANES):
            acc = None
            for cc in range(SSM_CHUNKS * LANES // SSM_WIDTH):
                ch = blk * (SSM_CHUNKS * LANES // SSM_WIDTH) + cc
                s = jnp.concatenate([re[pl.ds(ch * pitch, tc), :].astype(BF16),
                                     im[pl.ds(ch * pitch, tc), :].astype(BF16)], axis=1)
                d = jnp.dot(s, c_ref[ch], preferred_element_type=F32)
                acc = d if acc is None else acc + d
            y_ref[:, blk * LANES:(blk + 1) * LANES] = acc


def _s5_scan(u, w, tc=512):
    bsz, s, _ = u.shape
    tc = min(tc, s)
    n = s // tc
    pitch = tc + SUBLANES // 2
    rows = SSM_CHUNKS * pitch
    full = lambda a: pl.BlockSpec(a.shape, lambda b, c: (0,) * a.ndim)
    ublk = lambda f: pl.BlockSpec((None, tc, SSM_WIDTH), f)
    fwd = lambda b, c: (b, c, 0)
    bwd = lambda b, c: (b, n - 1 - c, 0)
    return pl.pallas_call(
        functools.partial(_s5_kernel, tc=tc, pitch=pitch),
        grid=(bsz, n),
        in_specs=[ublk(fwd), ublk(bwd), full(w['s5_wf']), full(w['s5_wb']),
                  full(w['s5_af']), full(w['s5_ab']), full(w['s5_cf']), full(w['s5_cb'])],
        out_specs=[ublk(fwd), ublk(bwd)],
        out_shape=[jax.ShapeDtypeStruct(u.shape, F32), jax.ShapeDtypeStruct(u.shape, F32)],
        scratch_shapes=[pltpu.VMEM((rows, LANES), F32) for _ in range(8)]
                       + [pltpu.VMEM((2, 4, SUBLANES, LANES), F32)],
        compiler_params=_params("parallel", "arbitrary"),
        name="s5_scan",
    )(u, u, w['s5_wf'], w['s5_wb'], w['s5_af'], w['s5_ab'], w['s5_cf'], w['s5_cb'])


def _mid_kernel(x_ref, att_ref, yf_ref, yb_ref, u_ref, dsk_ref, wglu_ref, bglu_ref,
                woa_ref, wos_ref, fn_ref, wqry_ref, x1_ref, xn_ref, qp_ref):
    y = dsk_ref[...] * u_ref[...] + yf_ref[...] + yb_ref[...]
    z = _gelu(y)
    gate = jnp.dot(z.astype(BF16), wglu_ref[...], preferred_element_type=F32) + bglu_ref[...]
    z = z * (1.0 / (1.0 + jnp.exp(-gate)))
    mix = (jnp.dot(att_ref[...], woa_ref[...], preferred_element_type=F32)
           + jnp.dot(z.astype(BF16), wos_ref[...], preferred_element_type=F32))
    x1 = x_ref[...] + mix
    x1_ref[...] = x1
    xn = _rms(x1, fn_ref[...]).astype(BF16)
    xn_ref[...] = xn
    qp_ref[...] = jnp.dot(xn, wqry_ref[...], preferred_element_type=F32).astype(BF16)


def _mid_stage(x, att, yf, yb, u, w, tt=512):
    n, d = x.shape
    tt = min(tt, n)
    full = lambda a: pl.BlockSpec(a.shape, lambda i: (0,) * a.ndim)
    row = lambda wd: pl.BlockSpec((tt, wd), lambda i: (i, 0))
    nq = w['w_query'].shape[1]
    return pl.pallas_call(
        _mid_kernel,
        grid=(n // tt,),
        in_specs=[row(d), row(att.shape[1]), row(SSM_WIDTH), row(SSM_WIDTH), row(SSM_WIDTH),
                  full(w['d_skip']), full(w['w_glu']), full(w['b_glu']), full(w['w_out_att']),
                  full(w['w_out_ssm']), full(w['ffn_norm']), full(w['w_query'])],
        out_specs=[row(d), row(d), row(nq)],
        out_shape=[jax.ShapeDtypeStruct((n, d), F32), jax.ShapeDtypeStruct((n, d), BF16),
                   jax.ShapeDtypeStruct((n, nq), BF16)],
        compiler_params=_params("parallel"),
        name="mid_stage",
    )(x, att, yf, yb, u, w['d_skip'], w['w_glu'], w['b_glu'], w['w_out_att'], w['w_out_ssm'],
      w['ffn_norm'], w['w_query'])


def _merge_desc(x):
    n = len(x)
    if n == 1:
        return x
    h = n // 2
    hi = [jnp.maximum(x[i], x[i + h]) for i in range(h)]
    lo = [jnp.minimum(x[i], x[i + h]) for i in range(h)]
    return _merge_desc(hi) + _merge_desc(lo)


def _sort_desc(x):
    n = len(x)
    if n == 1:
        return x
    h = n // 2
    return _merge_desc(_sort_desc(x[:h]) + _sort_desc(x[h:])[::-1])


def _merge_top(a, b):
    k = len(a)
    c = [jnp.maximum(a[i], b[k - 1 - i]) if k - 1 - i < len(b) else a[i] for i in range(k)]
    return _merge_desc(c)


def _route_kernel(qp_ref, keys_ref, n1_ref, e1_ref, r2_ref, e2_ref, sc_ref, out_ref, *, pitch):
    k_top = PEER_TOPK
    nk = PEER_NKEYS
    nsub = qp_ref.shape[0] // LANES

    for side in range(2):
        kmat = keys_ref[side]
        for j in range(nsub):
            qj = qp_ref[j * LANES:(j + 1) * LANES, side * LANES:(side + 1) * LANES]
            sc_ref[side, pl.ds(j * pitch, nk), :] = lax.dot_general(
                kmat, qj, (((1,), (1,)), ((), ())), preferred_element_type=F32)

    def key_rows(kk):
        return pl.ds(kk, nsub, stride=pitch)

    def top_sorted(side):
        best = None
        for g in range(nk // k_top):
            grp = _sort_desc([sc_ref[side, key_rows(g * k_top + i), :] for i in range(k_top)])
            best = grp if best is None else _merge_top(best, grp)
        return best

    v1 = top_sorted(0)
    v2 = top_sorted(1)

    lens = [k_top // (a + 1) for a in range(k_top)]
    cand = [[v1[a] + v2[b] for b in range(lens[a])] for a in range(k_top)]
    top = cand[0]
    a = 1
    while lens[a] > 1:
        top = _merge_top(top, cand[a])
        a += 1
    top = _merge_top(top, [cand[i][0] for i in range(a, k_top)])
    thr = top[k_top - 1]

    m = cand[0][0]
    z = None
    for row in cand:
        for c in row:
            e = jnp.where(c >= thr, jnp.exp(c - m), 0.0)
            z = e if z is None else z + e
    inv_z = 1.0 / z

    inf = jnp.full(thr.shape, jnp.inf, F32)
    tb = []
    for b in range(k_top):
        t = inf
        for a in range(k_top):
            if b < lens[a]:
                t = jnp.minimum(t, jnp.where(cand[a][b] >= thr, v1[a], inf))
        tb.append(t)

    for kk in range(nk):
        rows = key_rows(kk)
        s1 = sc_ref[0, rows, :]
        cnt = jnp.zeros_like(s1)
        for b in range(k_top):
            cnt = jnp.where(s1 >= tb[b], float(b + 1), cnt)
        out_ref[0, rows, :] = cnt
        out_ref[1, rows, :] = jnp.exp(s1 - v1[0]) * inv_z
        s2 = sc_ref[1, rows, :]
        rank = jnp.full(s2.shape, float(k_top), F32)
        for b in range(k_top - 1, -1, -1):
            rank = jnp.where(s2 >= v2[b], float(b), rank)
        out_ref[2, rows, :] = rank
        out_ref[3, rows, :] = jnp.exp(s2 - v2[0])

    for j in range(nsub):
        cols = slice(j * LANES, (j + 1) * LANES)
        rows = pl.ds(j * pitch, nk)
        n1_ref[:, cols] = out_ref[0, rows, :]
        e1_ref[:, cols] = out_ref[1, rows, :]
        r2_ref[:, cols] = out_ref[2, rows, :].astype(BF16)
        e2_ref[:, cols] = out_ref[3, rows, :].astype(BF16)


def _route(qp, keys):
    n = qp.shape[0]
    tt = min(SUBLANES * LANES, n)
    nsub = tt // LANES
    pitch = PEER_NKEYS + SUBLANES
    rows = nsub * pitch
    hk = 2 * LANES
    out_blk = pl.BlockSpec((None, PEER_NKEYS, tt), lambda i, h: (h, 0, i))
    shp = lambda dt: jax.ShapeDtypeStruct((PEER_HEADS, PEER_NKEYS, n), dt)
    return pl.pallas_call(
        functools.partial(_route_kernel, pitch=pitch),
        grid=(n // tt, PEER_HEADS),
        in_specs=[pl.BlockSpec((tt, hk), lambda i, h: (i, h)),
                  pl.BlockSpec((2, None, PEER_NKEYS, LANES), lambda i, h: (0, h, 0, 0))],
        out_specs=[out_blk, out_blk, out_blk, out_blk],
        out_shape=[shp(F32), shp(F32), shp(BF16), shp(BF16)],
        scratch_shapes=[pltpu.VMEM((2, rows, LANES), F32), pltpu.VMEM((4, rows, LANES), F32)],
        compiler_params=_params("parallel", "parallel"),
        name="peer_route",
    )(qp, keys)


def _peer_kernel(xn_ref, x1_ref, n1_ref, e1_ref, r2_ref, e2_ref, u_ref, vt_ref, fn_ref,
                 y_ref, a_ref, hid_ref, acc_ref, *, tcol):
    e = pl.program_id(1)
    tt = xn_ref.shape[0]
    rows_per_tile = u_ref.shape[0] // PEER_NKEYS

    @pl.when(e == 0)
    def _():
        acc_ref[...] = jnp.zeros(acc_ref.shape, F32)

    a_ref[...] = lax.dot_general(u_ref[...], xn_ref[...], (((1,), (1,)), ((), ())),
                                 preferred_element_type=F32)

    for i1 in range(rows_per_tile):
        row = i1
        for c in range(tt // tcol):
            cols = slice(c * tcol, (c + 1) * tcol)
            g = None
            for h in range(PEER_HEADS):
                n1 = jnp.broadcast_to(n1_ref[h, pl.ds(row, 1), cols].astype(BF16), (PEER_NKEYS, tcol))
                e1 = jnp.broadcast_to(e1_ref[h, pl.ds(row, 1), cols].astype(BF16), (PEER_NKEYS, tcol))
                sel = jnp.where(r2_ref[h, :, cols] < n1, e2_ref[h, :, cols], jnp.zeros((), BF16)) * e1
                g = sel if g is None else g + sel
            a = a_ref[i1 * PEER_NKEYS:(i1 + 1) * PEER_NKEYS, cols].astype(BF16)
            hid_ref[i1 * PEER_NKEYS:(i1 + 1) * PEER_NKEYS, cols] = _gelu(a) * g

    acc_ref[...] += jnp.dot(vt_ref[...], hid_ref[...], preferred_element_type=F32)

    @pl.when(e == pl.num_programs(1) - 1)
    def _():
        x2 = x1_ref[...] + acc_ref[...].T
        y_ref[...] = _rms(x2, fn_ref[...])


def _peer(xn, x1, n1, e1, r2, e2, w, tt=1024, te=1024, tcol=256):
    n, d = xn.shape
    tt = min(tt, n)
    tcol = min(tcol, tt)
    ne = w['expert_u'].shape[0]
    rt = lambda: pl.BlockSpec((PEER_HEADS, PEER_NKEYS, tt), lambda i, e: (0, 0, i))
    r1 = lambda: pl.BlockSpec((PEER_HEADS, te // PEER_NKEYS, tt), lambda i, e: (0, e, i))
    return pl.pallas_call(
        functools.partial(_peer_kernel, tcol=tcol),
        grid=(n // tt, ne // te),
        in_specs=[pl.BlockSpec((tt, d), lambda i, e: (i, 0)),
                  pl.BlockSpec((tt, d), lambda i, e: (i, 0)),
                  r1(), r1(), rt(), rt(),
                  pl.BlockSpec((te, d), lambda i, e: (e, 0)),
                  pl.BlockSpec((d, te), lambda i, e: (0, e)),
                  pl.BlockSpec((1, d), lambda i, e: (0, 0))],
        out_specs=pl.BlockSpec((tt, d), lambda i, e: (i, 0)),
        out_shape=jax.ShapeDtypeStruct((n, d), F32),
        scratch_shapes=[pltpu.VMEM((te, tt), F32), pltpu.VMEM((te, tt), BF16),
                        pltpu.VMEM((d, tt), F32)],
        compiler_params=_params("parallel", "arbitrary"),
        name="peer_experts",
    )(xn, x1, n1, e1, r2, e2, w['expert_u'], w['expert_vt'], w['final_norm'])


def _rot_cols(wm):
    hw = wm.shape[-1] // 2
    return jnp.concatenate([-wm[..., hw:], wm[..., :hw]], axis=-1)


def _s5_direction_params(lam_re, lam_im, log_dt, b_re, b_im, c_re, c_im):
    g, p, ch = SSM_GROUPS, SSM_STATE, SSM_GROUP_CH
    dt = jnp.exp(log_dt)[:, None]
    mag = jnp.exp(lam_re * dt)
    a_re = mag * jnp.cos(lam_im * dt)
    a_im = mag * jnp.sin(lam_im * dt)
    nr = a_re - 1.0
    den = lam_re * lam_re + lam_im * lam_im
    f_re = (nr * lam_re + a_im * lam_im) / den
    f_im = (a_im * lam_re - nr * lam_im) / den
    bb_re = f_re[..., None] * b_re - f_im[..., None] * b_im
    bb_im = f_re[..., None] * b_im + f_im[..., None] * b_re
    eye = jnp.eye(g, dtype=F32)
    per_blk = SSM_CHUNKS * LANES // SSM_WIDTH

    def expand(bb):
        nb = SSM_WIDTH // LANES
        m = jnp.einsum('gph,gk->ghkp', bb, eye).reshape(nb, LANES, nb, per_blk, LANES)
        return jnp.einsum('arack->acrk', m).reshape(SSM_CHUNKS, LANES, LANES)

    def readout(cm):
        nb = SSM_WIDTH // LANES
        m = jnp.einsum('ghp,gk->gpkh', cm, eye).reshape(nb, per_blk, LANES, nb, LANES)
        return jnp.einsum('acrak->acrk', m).reshape(SSM_CHUNKS, LANES, LANES)

    wexp = jnp.concatenate([expand(bb_re), expand(bb_im)], axis=-1).astype(BF16)
    cmat = jnp.concatenate([readout(c_re), readout(-c_im)], axis=1).astype(BF16)
    a = jnp.concatenate([a_re.reshape(2, SUBLANES, LANES), a_im.reshape(2, SUBLANES, LANES)])
    return wexp, a, cmat


def _prepare(mix_norm, w_in, q_norm, kv_norm, w_uq, w_ukv, lam_re, lam_im, log_dt, b_re, b_im,
             c_re, c_im, d_skip, w_glu, b_glu, w_out, ffn_norm, w_query, sub_keys, expert_u,
             expert_v, final_norm):
    o1 = Q_LORA
    o2 = o1 + KV_LORA
    o3 = o2 + QK_ROPE
    wi = w_in[0]
    w_kpe = wi[:, o2:o3]
    w = {}
    w['mix_norm'] = mix_norm[0][None, :]
    w['w_in'] = jnp.concatenate([wi[:, :o2], w_kpe, _rot_cols(w_kpe), wi[:, o3:]], axis=1).astype(BF16)
    w['q_norm'] = q_norm[0][None, :]
    w['kv_norm'] = kv_norm[0][None, :]
    wq = w_uq[0]
    wq_pe = wq[..., QK_NOPE:]
    w['w_q'] = jnp.concatenate([wq[..., :QK_NOPE], wq_pe, _rot_cols(wq_pe)], axis=-1).reshape(
        Q_LORA, N_HEADS * QK_PAD).astype(BF16)
    w['w_kv'] = w_ukv[0].reshape(KV_LORA, N_HEADS * (QK_NOPE + V_HEAD)).astype(BF16)
    for name, d in (('f', 0), ('b', 1)):
        wexp, a, cmat = _s5_direction_params(lam_re[0, d], lam_im[0, d], log_dt[0, d], b_re[0, d],
                                             b_im[0, d], c_re[0, d], c_im[0, d])
        w['s5_w' + name], w['s5_a' + name], w['s5_c' + name] = wexp, a, cmat
    w['d_skip'] = d_skip[0].reshape(1, SSM_WIDTH)
    w['w_glu'] = w_glu[0].astype(BF16)
    w['b_glu'] = b_glu[0][None, :]
    mla_w = N_HEADS * V_HEAD
    w['w_out_att'] = w_out[0][:mla_w].astype(BF16)
    w['w_out_ssm'] = w_out[0][mla_w:].astype(BF16)
    w['ffn_norm'] = ffn_norm[0][None, :]
    w['w_query'] = w_query[0].astype(BF16)
    w['sub_keys'] = sub_keys[0].astype(BF16)
    w['expert_u'] = expert_u[0].astype(BF16)
    w['expert_vt'] = expert_v[0].astype(BF16).T
    w['final_norm'] = final_norm[None, :]
    return w


def _rope_table(s):
    inv = ROPE_THETA ** (-jnp.arange(0, QK_ROPE, 2, dtype=F32) / QK_ROPE)
    ang = jnp.arange(s, dtype=F32)[:, None] * inv[None, :]
    c, sn = jnp.cos(ang), jnp.sin(ang)
    return jnp.concatenate([c, c, sn, sn], axis=1)


def _encoder(x, w):
    bsz, s, d = x.shape
    q, k, v, u = _in_stage(x, _rope_table(s), w)
    att = _attention(q, k, v)
    yf, yb = _s5_scan(u, w)
    n = bsz * s
    x1, xn, qp = _mid_stage(x.reshape(n, d), att.reshape(n, -1), yf.reshape(n, -1),
                            yb.reshape(n, -1), u.reshape(n, -1), w)
    n1, e1, r2, e2 = _route(qp, w['sub_keys'])
    y = _peer(xn, x1, n1, e1, r2, e2, w)
    return y.reshape(bsz, s, d)


def kernel(x_prompt, x_sample, mix_norm, w_in, q_norm, kv_norm, w_uq, w_ukv, lam_re, lam_im, log_dt, b_re, b_im, c_re, c_im, d_skip, w_glu, b_glu, w_out, ffn_norm, w_query, sub_keys, expert_u, expert_v, final_norm):
    w = _prepare(mix_norm, w_in, q_norm, kv_norm, w_uq, w_ukv, lam_re, lam_im, log_dt, b_re, b_im,
                 c_re, c_im, d_skip, w_glu, b_glu, w_out, ffn_norm, w_query, sub_keys, expert_u,
                 expert_v, final_norm)
    return (_encoder(x_prompt, w), _encoder(x_sample, w))
```

```python
import functools
import math

import jax
import jax.numpy as jnp
from jax import lax
from jax.experimental import pallas as pl
from jax.experimental.pallas import tpu as pltpu

F32 = jnp.float32
BF16 = jnp.bfloat16

EPS = 1e-6
ROPE_THETA = 10000.0
LANES = 128
SUBLANES = 8
VMEM_LIMIT = 56 * 1024 * 1024

N_HEADS = 4
QK_NOPE = 128
QK_ROPE = 64
V_HEAD = 128
QK_PAD = 256
Q_LORA = 384
KV_LORA = 256
SSM_WIDTH = 512
SSM_GROUPS = 32
SSM_GROUP_CH = 16
SSM_STATE = 64
SSM_CHUNKS = SSM_GROUPS * SSM_STATE // LANES
PEER_HEADS = 8
PEER_NKEYS = 128
PEER_TOPK = 16
GELU_C = math.sqrt(2.0 / math.pi)


def _gelu(x):
    return 0.5 * x * (1.0 + jnp.tanh(GELU_C * (x + 0.044715 * (x * x * x))))


def _rms(x, g):
    return x * lax.rsqrt(jnp.mean(x * x, axis=-1, keepdims=True) + EPS) * g


def _params(*sem):
    return pltpu.CompilerParams(dimension_semantics=sem, vmem_limit_bytes=VMEM_LIMIT)


def _in_kernel(x_ref, cs_ref, g_ref, win_ref, qn_ref, kvn_ref, wq_ref, wkv_ref,
               q_ref, k_ref, v_ref, u_ref):
    x = x_ref[...]
    h = _rms(x, g_ref[...])
    proj = jnp.dot(h.astype(BF16), win_ref[...], preferred_element_type=F32)
    o1 = Q_LORA
    o2 = o1 + KV_LORA
    o3 = o2 + 2 * QK_ROPE
    cq = proj[:, :o1]
    ckv = proj[:, o1:o2]
    kp = proj[:, o2:o3]
    u_ref[...] = proj[:, o3:]
    cs = cs_ref[...]
    lane = lax.broadcasted_iota(jnp.int32, kp.shape, 1)
    t = kp * cs
    kpe = jnp.where(lane < QK_ROPE, t + pltpu.roll(t, QK_ROPE, axis=1), 0.0)
    scale = (QK_NOPE + QK_ROPE) ** -0.5
    q = jnp.dot(_rms(cq, qn_ref[...]).astype(BF16), wq_ref[...], preferred_element_type=F32) * scale
    kv = jnp.dot(_rms(ckv, kvn_ref[...]).astype(BF16), wkv_ref[...], preferred_element_type=F32)
    for hd in range(N_HEADS):
        b = hd * QK_PAD
        tq = q[:, b + QK_NOPE:b + QK_PAD] * cs
        q_ref[hd, :, :QK_NOPE] = q[:, b:b + QK_NOPE].astype(BF16)
        q_ref[hd, :, QK_NOPE:] = (tq + pltpu.roll(tq, QK_ROPE, axis=1)).astype(BF16)
        k_ref[hd, :, :QK_NOPE] = kv[:, b:b + QK_NOPE].astype(BF16)
        k_ref[hd, :, QK_NOPE:] = kpe.astype(BF16)
        v_ref[hd] = kv[:, b + QK_NOPE:b + QK_PAD].astype(BF16)


def _in_stage(x, cs, w, tt=512):
    bsz, s, d = x.shape
    tt = min(tt, s)
    grid = (bsz, s // tt)
    full = lambda a: pl.BlockSpec(a.shape, lambda b, i: (0,) * a.ndim)
    hs = lambda wd: pl.BlockSpec((None, N_HEADS, tt, wd), lambda b, i: (b, 0, i, 0))
    return pl.pallas_call(
        _in_kernel,
        grid=grid,
        in_specs=[pl.BlockSpec((None, tt, d), lambda b, i: (b, i, 0)),
                  pl.BlockSpec((tt, 2 * QK_ROPE), lambda b, i: (i, 0)),
                  full(w['mix_norm']), full(w['w_in']), full(w['q_norm']), full(w['kv_norm']),
                  full(w['w_q']), full(w['w_kv'])],
        out_specs=[hs(QK_PAD), hs(QK_PAD), hs(V_HEAD),
                   pl.BlockSpec((None, tt, SSM_WIDTH), lambda b, i: (b, i, 0))],
        out_shape=[jax.ShapeDtypeStruct((bsz, N_HEADS, s, QK_PAD), BF16),
                   jax.ShapeDtypeStruct((bsz, N_HEADS, s, QK_PAD), BF16),
                   jax.ShapeDtypeStruct((bsz, N_HEADS, s, V_HEAD), BF16),
                   jax.ShapeDtypeStruct((bsz, s, SSM_WIDTH), F32)],
        compiler_params=_params("parallel", "parallel"),
        name="in_stage",
    )(x, cs, w['mix_norm'], w['w_in'], w['q_norm'], w['kv_norm'], w['w_q'], w['w_kv'])


def _flash_kernel(q_ref, k_ref, v_ref, o_ref, m_ref, acc_ref, s0, s1, p0, p1, al0, al1, *, tk):
    n = k_ref.shape[0] // tk
    m_ref[...] = jnp.full(m_ref.shape, -jnp.inf, F32)
    acc_ref[...] = jnp.zeros(acc_ref.shape, F32)
    ones = jnp.ones((tk, LANES), BF16)
    s_buf, p_buf, al_buf = (s0, s1), (p0, p1), (al0, al1)

    def scores(j, slot):
        kj = k_ref[pl.ds(pl.multiple_of(j * tk, tk), tk), :]
        s_buf[slot][...] = lax.dot_general(q_ref[...], kj, (((1,), (1,)), ((), ())),
                                           preferred_element_type=F32)

    def exps(slot):
        s = s_buf[slot][...]
        m_prev = m_ref[...]
        m_new = jnp.maximum(m_prev, jnp.max(s, axis=1, keepdims=True))
        p_buf[slot][...] = jnp.exp(s - jnp.concatenate([m_new] * (tk // LANES), axis=1)).astype(BF16)
        al_buf[slot][...] = jnp.exp(m_prev - m_new)
        m_ref[...] = m_new

    def values(j, slot):
        vj = jnp.concatenate([v_ref[pl.ds(pl.multiple_of(j * tk, tk), tk), :], ones], axis=1)
        acc_ref[...] = (jnp.concatenate([al_buf[slot][...]] * 2, axis=1) * acc_ref[...]
                        + jnp.dot(p_buf[slot][...], vj, preferred_element_type=F32))

    scores(0, 0)
    scores(1, 1)
    exps(0)

    def body(jj, carry):
        j = 2 * jj + 2
        scores(j, 0)
        exps(1)
        values(j - 2, 0)
        scores(j + 1, 1)
        exps(0)
        values(j - 1, 1)
        return carry

    lax.fori_loop(0, (n - 2) // 2, body, 0, unroll=True)
    exps(1)
    values(n - 2, 0)
    values(n - 1, 1)
    o_ref[...] = (acc_ref[:, :V_HEAD] / acc_ref[:, V_HEAD:]).astype(o_ref.dtype)


def _attention(q, k, v, tq=1024, tk=2048):
    bsz, nh, s, _ = q.shape
    tq = min(tq, s)
    tk = min(tk, s // 2)
    assert s % (2 * tk) == 0 and s % tq == 0
    return pl.pallas_call(
        functools.partial(_flash_kernel, tk=tk),
        grid=(bsz, nh, s // tq),
        in_specs=[pl.BlockSpec((None, None, tq, QK_PAD), lambda b, h, i: (b, h, i, 0)),
                  pl.BlockSpec((None, None, s, QK_PAD), lambda b, h, i: (b, h, 0, 0)),
                  pl.BlockSpec((None, None, s, V_HEAD), lambda b, h, i: (b, h, 0, 0))],
        out_specs=pl.BlockSpec((None, tq, V_HEAD), lambda b, h, i: (b, i, h)),
        out_shape=jax.ShapeDtypeStruct((bsz, s, nh * V_HEAD), BF16),
        scratch_shapes=[pltpu.VMEM((tq, LANES), F32), pltpu.VMEM((tq, 2 * V_HEAD), F32),
                        pltpu.VMEM((tq, tk), F32), pltpu.VMEM((tq, tk), F32),
                        pltpu.VMEM((tq, tk), BF16), pltpu.VMEM((tq, tk), BF16),
                        pltpu.VMEM((tq, LANES), F32), pltpu.VMEM((tq, LANES), F32)],
        compiler_params=_params("parallel", "parallel", "arbitrary"),
        name="attention",
    )(q, k, v)


def _s5_kernel(uf_ref, ub_ref, wf_ref, wb_ref, af_ref, ab_ref, cf_ref, cb_ref,
               yf_ref, yb_ref, fre, fim, bre, bim, sfre, sfim, sbre, sbim, st_ref, *, tc, pitch):
    half = SSM_CHUNKS // 2 * pitch

    @pl.when(pl.program_id(1) == 0)
    def _():
        st_ref[...] = jnp.zeros(st_ref.shape, F32)

    for u_ref, w_ref, re, im in ((uf_ref, wf_ref, fre, fim), (ub_ref, wb_ref, bre, bim)):
        for blk in range(SSM_WIDTH // LANES):
            ub = u_ref[:, blk * LANES:(blk + 1) * LANES].astype(BF16)
            for cc in range(SSM_CHUNKS * LANES // SSM_WIDTH):
                ch = blk * (SSM_CHUNKS * LANES // SSM_WIDTH) + cc
                r = jnp.dot(ub, w_ref[ch], preferred_element_type=F32)
                re[pl.ds(ch * pitch, tc), :] = r[:, :LANES]
                im[pl.ds(ch * pitch, tc), :] = r[:, LANES:]

    af = af_ref[...]
    ab = ab_ref[...]

    def step(a, st, re, im, sre, sim, t):
        out = []
        for hf in range(2):
            rows = pl.ds(hf * half + t, SUBLANES, stride=pitch)
            a_re, a_im = a[hf], a[2 + hf]
            s_re, s_im = st[hf], st[2 + hf]
            n_re = a_re * s_re - a_im * s_im + re[rows, :]
            n_im = a_re * s_im + a_im * s_re + im[rows, :]
            sre[rows, :] = n_re
            sim[rows, :] = n_im
            out.append((n_re, n_im))
        return (out[0][0], out[1][0], out[0][1], out[1][1])

    def body(t, carry):
        sf, sb = carry
        sf = step(af, sf, fre, fim, sfre, sfim, t)
        sb = step(ab, sb, bre, bim, sbre, sbim, tc - 1 - t)
        return (sf, sb)

    st = st_ref[...]
    init = (tuple(st[0, i] for i in range(4)), tuple(st[1, i] for i in range(4)))
    sf, sb = lax.fori_loop(0, tc, body, init, unroll=8)
    for i in range(4):
        st_ref[0, i] = sf[i]
        st_ref[1, i] = sb[i]

    for y_ref, c_ref, re, im in ((yf_ref, cf_ref, sfre, sfim), (yb_ref, cb_ref, sbre, sbim)):
        for blk in range(SSM_WIDTH // LANES):
            acc = None
            for cc in range(SSM_CHUNKS * LANES // SSM_WIDTH):
                ch = blk * (SSM_CHUNKS * LANES // SSM_WIDTH) + cc
                s = jnp.concatenate([re[pl.ds(ch * pitch, tc), :].astype(BF16),
                                     im[pl.ds(ch * pitch, tc), :].astype(BF16)], axis=1)
                d = jnp.dot(s, c_ref[ch], preferred_element_type=F32)
                acc = d if acc is None else acc + d
            y_ref[:, blk * LANES:(blk + 1) * LANES] = acc


def _s5_scan(u, w, tc=512):
    bsz, s, _ = u.shape
    tc = min(tc, s)
    n = s // tc
    pitch = tc + SUBLANES // 2
    rows = SSM_CHUNKS * pitch
    full = lambda a: pl.BlockSpec(a.shape, lambda b, c: (0,) * a.ndim)
    ublk = lambda f: pl.BlockSpec((None, tc, SSM_WIDTH), f)
    fwd = lambda b, c: (b, c, 0)
    bwd = lambda b, c: (b, n - 1 - c, 0)
    return pl.pallas_call(
        functools.partial(_s5_kernel, tc=tc, pitch=pitch),
        grid=(bsz, n),
        in_specs=[ublk(fwd), ublk(bwd), full(w['s5_wf']), full(w['s5_wb']),
                  full(w['s5_af']), full(w['s5_ab']), full(w['s5_cf']), full(w['s5_cb'])],
        out_specs=[ublk(fwd), ublk(bwd)],
        out_shape=[jax.ShapeDtypeStruct(u.shape, F32), jax.ShapeDtypeStruct(u.shape, F32)],
        scratch_shapes=[pltpu.VMEM((rows, LANES), F32) for _ in range(8)]
                       + [pltpu.VMEM((2, 4, SUBLANES, LANES), F32)],
        compiler_params=_params("parallel", "arbitrary"),
        name="s5_scan",
    )(u, u, w['s5_wf'], w['s5_wb'], w['s5_af'], w['s5_ab'], w['s5_cf'], w['s5_cb'])


def _mid_kernel(x_ref, att_ref, yf_ref, yb_ref, u_ref, dsk_ref, wglu_ref, bglu_ref,
                woa_ref, wos_ref, fn_ref, wqry_ref, x1_ref, xn_ref, qp_ref):
    y = dsk_ref[...] * u_ref[...] + yf_ref[...] + yb_ref[...]
    z = _gelu(y)
    gate = jnp.dot(z.astype(BF16), wglu_ref[...], preferred_element_type=F32) + bglu_ref[...]
    z = z * (1.0 / (1.0 + jnp.exp(-gate)))
    mix = (jnp.dot(att_ref[...], woa_ref[...], preferred_element_type=F32)
           + jnp.dot(z.astype(BF16), wos_ref[...], preferred_element_type=F32))
    x1 = x_ref[...] + mix
    x1_ref[...] = x1
    xn = _rms(x1, fn_ref[...]).astype(BF16)
    xn_ref[...] = xn
    qp_ref[...] = jnp.dot(xn, wqry_ref[...], preferred_element_type=F32).astype(BF16)


def _mid_stage(x, att, yf, yb, u, w, tt=512):
    n, d = x.shape
    tt = min(tt, n)
    full = lambda a: pl.BlockSpec(a.shape, lambda i: (0,) * a.ndim)
    row = lambda wd: pl.BlockSpec((tt, wd), lambda i: (i, 0))
    nq = w['w_query'].shape[1]
    return pl.pallas_call(
        _mid_kernel,
        grid=(n // tt,),
        in_specs=[row(d), row(att.shape[1]), row(SSM_WIDTH), row(SSM_WIDTH), row(SSM_WIDTH),
                  full(w['d_skip']), full(w['w_glu']), full(w['b_glu']), full(w['w_out_att']),
                  full(w['w_out_ssm']), full(w['ffn_norm']), full(w['w_query'])],
        out_specs=[row(d), row(d), row(nq)],
        out_shape=[jax.ShapeDtypeStruct((n, d), F32), jax.ShapeDtypeStruct((n, d), BF16),
                   jax.ShapeDtypeStruct((n, nq), BF16)],
        compiler_params=_params("parallel"),
        name="mid_stage",
    )(x, att, yf, yb, u, w['d_skip'], w['w_glu'], w['b_glu'], w['w_out_att'], w['w_out_ssm'],
      w['ffn_norm'], w['w_query'])


def _merge_desc(x):
    n = len(x)
    if n == 1:
        return x
    h = n // 2
    hi = [jnp.maximum(x[i], x[i + h]) for i in range(h)]
    lo = [jnp.minimum(x[i], x[i + h]) for i in range(h)]
    return _merge_desc(hi) + _merge_desc(lo)


def _sort_desc(x):
    n = len(x)
    if n == 1:
        return x
    h = n // 2
    return _merge_desc(_sort_desc(x[:h]) + _sort_desc(x[h:])[::-1])


def _merge_top(a, b):
    k = len(a)
    c = [jnp.maximum(a[i], b[k - 1 - i]) if k - 1 - i < len(b) else a[i] for i in range(k)]
    return _merge_desc(c)


def _route_kernel(qp_ref, keys_ref, n1_ref, e1_ref, r2_ref, e2_ref, sc_ref, out_ref, *, pitch):
    k_top = PEER_TOPK
    nk = PEER_NKEYS
    nsub = qp_ref.shape[0] // LANES

    for side in range(2):
        kmat = keys_ref[side]
        for j in range(nsub):
            qj = qp_ref[j * LANES:(j + 1) * LANES, side * LANES:(side + 1) * LANES]
            sc_ref[side, pl.ds(j * pitch, nk), :] = lax.dot_general(
                kmat, qj, (((1,), (1,)), ((), ())), preferred_element_type=F32)

    def key_rows(kk):
        return pl.ds(kk, nsub, stride=pitch)

    def top_sorted(side):
        best = None
        for g in range(nk // k_top):
            grp = _sort_desc([sc_ref[side, key_rows(g * k_top + i), :] for i in range(k_top)])
            best = grp if best is None else _merge_top(best, grp)
        return best

    v1 = top_sorted(0)
    v2 = top_sorted(1)

    lens = [k_top // (a + 1) for a in range(k_top)]
    cand = [[v1[a] + v2[b] for b in range(lens[a])] for a in range(k_top)]
    top = cand[0]
    a = 1
    while lens[a] > 1:
        top = _merge_top(top, cand[a])
        a += 1
    top = _merge_top(top, [cand[i][0] for i in range(a, k_top)])
    thr = top[k_top - 1]

    m = cand[0][0]
    z = None
    for row in cand:
        for c in row:
            e = jnp.where(c >= thr, jnp.exp(c - m), 0.0)
            z = e if z is None else z + e
    inv_z = 1.0 / z

    inf = jnp.full(thr.shape, jnp.inf, F32)
    tb = []
    for b in range(k_top):
        t = inf
        for a in range(k_top):
            if b < lens[a]:
                t = jnp.minimum(t, jnp.where(cand[a][b] >= thr, v1[a], inf))
        tb.append(t)

    for kk in range(nk):
        rows = key_rows(kk)
        s1 = sc_ref[0, rows, :]
        cnt = jnp.zeros_like(s1)
        for b in range(k_top):
            cnt = jnp.where(s1 >= tb[b], float(b + 1), cnt)
        out_ref[0, rows, :] = cnt
        out_ref[1, rows, :] = jnp.exp(s1 - v1[0]) * inv_z
        s2 = sc_ref[1, rows, :]
        rank = jnp.full(s2.shape, float(k_top), F32)
        for b in range(k_top - 1, -1, -1):
            rank = jnp.where(s2 >= v2[b], float(b), rank)
        out_ref[2, rows, :] = rank
        out_ref[3, rows, :] = jnp.exp(s2 - v2[0])

    for j in range(nsub):
        cols = slice(j * LANES, (j + 1) * LANES)
        rows = pl.ds(j * pitch, nk)
        n1_ref[:, cols] = out_ref[0, rows, :]
        e1_ref[:, cols] = out_ref[1, rows, :]
        r2_ref[:, cols] = out_ref[2, rows, :].astype(BF16)
        e2_ref[:, cols] = out_ref[3, rows, :].astype(BF16)


def _route(qp, keys):
    n = qp.shape[0]
    tt = min(SUBLANES * LANES, n)
    nsub = tt // LANES
    pitch = PEER_NKEYS + SUBLANES
    rows = nsub * pitch
    hk = 2 * LANES
    out_blk = pl.BlockSpec((None, PEER_NKEYS, tt), lambda i, h: (h, 0, i))
    shp = lambda dt: jax.ShapeDtypeStruct((PEER_HEADS, PEER_NKEYS, n), dt)
    return pl.pallas_call(
        functools.partial(_route_kernel, pitch=pitch),
        grid=(n // tt, PEER_HEADS),
        in_specs=[pl.BlockSpec((tt, hk), lambda i, h: (i, h)),
                  pl.BlockSpec((2, None, PEER_NKEYS, LANES), lambda i, h: (0, h, 0, 0))],
        out_specs=[out_blk, out_blk, out_blk, out_blk],
        out_shape=[shp(F32), shp(F32), shp(BF16), shp(BF16)],
        scratch_shapes=[pltpu.VMEM((2, rows, LANES), F32), pltpu.VMEM((4, rows, LANES), F32)],
        compiler_params=_params("parallel", "parallel"),
        name="peer_route",
    )(qp, keys)


def _peer_kernel(xn_ref, x1_ref, n1_ref, e1_ref, r2_ref, e2_ref, u_ref, vt_ref, fn_ref,
                 y_ref, a_ref, hid_ref, acc_ref, *, tcol):
    e = pl.program_id(1)
    tt = xn_ref.shape[0]
    rows_per_tile = u_ref.shape[0] // PEER_NKEYS

    @pl.when(e == 0)
    def _():
        acc_ref[...] = jnp.zeros(acc_ref.shape, F32)

    a_ref[...] = lax.dot_general(u_ref[...], xn_ref[...], (((1,), (1,)), ((), ())),
                                 preferred_element_type=F32)

    for i1 in range(rows_per_tile):
        row = i1
        for c in range(tt // tcol):
            cols = slice(c * tcol, (c + 1) * tcol)
            g = None
            for h in range(PEER_HEADS):
                n1 = jnp.broadcast_to(n1_ref[h, pl.ds(row, 1), cols].astype(BF16), (PEER_NKEYS, tcol))
                e1 = jnp.broadcast_to(e1_ref[h, pl.ds(row, 1), cols].astype(BF16), (PEER_NKEYS, tcol))
                sel = jnp.where(r2_ref[h, :, cols] < n1, e2_ref[h, :, cols], jnp.zeros((), BF16)) * e1
                g = sel if g is None else g + sel
            a = a_ref[i1 * PEER_NKEYS:(i1 + 1) * PEER_NKEYS, cols].astype(BF16)
            hid_ref[i1 * PEER_NKEYS:(i1 + 1) * PEER_NKEYS, cols] = _gelu(a) * g

    acc_ref[...] += jnp.dot(vt_ref[...], hid_ref[...], preferred_element_type=F32)

    @pl.when(e == pl.num_programs(1) - 1)
    def _():
        x2 = x1_ref[...] + acc_ref[...].T
        y_ref[...] = _rms(x2, fn_ref[...])


def _peer(xn, x1, n1, e1, r2, e2, w, tt=1024, te=1024, tcol=256):
    n, d = xn.shape
    tt = min(tt, n)
    tcol = min(tcol, tt)
    ne = w['expert_u'].shape[0]
    rt = lambda: pl.BlockSpec((PEER_HEADS, PEER_NKEYS, tt), lambda i, e: (0, 0, i))
    r1 = lambda: pl.BlockSpec((PEER_HEADS, te // PEER_NKEYS, tt), lambda i, e: (0, e, i))
    return pl.pallas_call(
        functools.partial(_peer_kernel, tcol=tcol),
        grid=(n // tt, ne // te),
        in_specs=[pl.BlockSpec((tt, d), lambda i, e: (i, 0)),
                  pl.BlockSpec((tt, d), lambda i, e: (i, 0)),
                  r1(), r1(), rt(), rt(),
                  pl.BlockSpec((te, d), lambda i, e: (e, 0)),
                  pl.BlockSpec((d, te), lambda i, e: (0, e)),
                  pl.BlockSpec((1, d), lambda i, e: (0, 0))],
        out_specs=pl.BlockSpec((tt, d), lambda i, e: (i, 0)),
        out_shape=jax.ShapeDtypeStruct((n, d), F32),
        scratch_shapes=[pltpu.VMEM((te, tt), F32), pltpu.VMEM((te, tt), BF16),
                        pltpu.VMEM((d, tt), F32)],
        compiler_params=_params("parallel", "arbitrary"),
        name="peer_experts",
    )(xn, x1, n1, e1, r2, e2, w['expert_u'], w['expert_vt'], w['final_norm'])


def _rot_cols(wm):
    hw = wm.shape[-1] // 2
    return jnp.concatenate([-wm[..., hw:], wm[..., :hw]], axis=-1)


def _s5_direction_params(lam_re, lam_im, log_dt, b_re, b_im, c_re, c_im):
    g, p, ch = SSM_GROUPS, SSM_STATE, SSM_GROUP_CH
    dt = jnp.exp(log_dt)[:, None]
    mag = jnp.exp(lam_re * dt)
    a_re = mag * jnp.cos(lam_im * dt)
    a_im = mag * jnp.sin(lam_im * dt)
    nr = a_re - 1.0
    den = lam_re * lam_re + lam_im * lam_im
    f_re = (nr * lam_re + a_im * lam_im) / den
    f_im = (a_im * lam_re - nr * lam_im) / den
    bb_re = f_re[..., None] * b_re - f_im[..., None] * b_im
    bb_im = f_re[..., None] * b_im + f_im[..., None] * b_re
    eye = jnp.eye(g, dtype=F32)
    per_blk = SSM_CHUNKS * LANES // SSM_WIDTH

    def expand(bb):
        nb = SSM_WIDTH // LANES
        m = jnp.einsum('gph,gk->ghkp', bb, eye).reshape(nb, LANES, nb, per_blk, LANES)
        return jnp.einsum('arack->acrk', m).reshape(SSM_CHUNKS, LANES, LANES)

    def readout(cm):
        nb = SSM_WIDTH // LANES
        m = jnp.einsum('ghp,gk->gpkh', cm, eye).reshape(nb, per_blk, LANES, nb, LANES)
        return jnp.einsum('acrak->acrk', m).reshape(SSM_CHUNKS, LANES, LANES)

    wexp = jnp.concatenate([expand(bb_re), expand(bb_im)], axis=-1).astype(BF16)
    cmat = jnp.concatenate([readout(c_re), readout(-c_im)], axis=1).astype(BF16)
    a = jnp.concatenate([a_re.reshape(2, SUBLANES, LANES), a_im.reshape(2, SUBLANES, LANES)])
    return wexp, a, cmat


def _prepare(mix_norm, w_in, q_norm, kv_norm, w_uq, w_ukv, lam_re, lam_im, log_dt, b_re, b_im,
             c_re, c_im, d_skip, w_glu, b_glu, w_out, ffn_norm, w_query, sub_keys, expert_u,
             expert_v, final_norm):
    o1 = Q_LORA
    o2 = o1 + KV_LORA
    o3 = o2 + QK_ROPE
    wi = w_in[0]
    w_kpe = wi[:, o2:o3]
    w = {}
    w['mix_norm'] = mix_norm[0][None, :]
    w['w_in'] = jnp.concatenate([wi[:, :o2], w_kpe, _rot_cols(w_kpe), wi[:, o3:]], axis=1).astype(BF16)
    w['q_norm'] = q_norm[0][None, :]
    w['kv_norm'] = kv_norm[0][None, :]
    wq = w_uq[0]
    wq_pe = wq[..., QK_NOPE:]
    w['w_q'] = jnp.concatenate([wq[..., :QK_NOPE], wq_pe, _rot_cols(wq_pe)], axis=-1).reshape(
        Q_LORA, N_HEADS * QK_PAD).astype(BF16)
    w['w_kv'] = w_ukv[0].reshape(KV_LORA, N_HEADS * (QK_NOPE + V_HEAD)).astype(BF16)
    for name, d in (('f', 0), ('b', 1)):
        wexp, a, cmat = _s5_direction_params(lam_re[0, d], lam_im[0, d], log_dt[0, d], b_re[0, d],
                                             b_im[0, d], c_re[0, d], c_im[0, d])
        w['s5_w' + name], w['s5_a' + name], w['s5_c' + name] = wexp, a, cmat
    w['d_skip'] = d_skip[0].reshape(1, SSM_WIDTH)
    w['w_glu'] = w_glu[0].astype(BF16)
    w['b_glu'] = b_glu[0][None, :]
    mla_w = N_HEADS * V_HEAD
    w['w_out_att'] = w_out[0][:mla_w].astype(BF16)
    w['w_out_ssm'] = w_out[0][mla_w:].astype(BF16)
    w['ffn_norm'] = ffn_norm[0][None, :]
    w['w_query'] = w_query[0].astype(BF16)
    w['sub_keys'] = sub_keys[0].astype(BF16)
    w['expert_u'] = expert_u[0].astype(BF16)
    w['expert_vt'] = expert_v[0].astype(BF16).T
    w['final_norm'] = final_norm[None, :]
    return w


def _rope_table(s):
    inv = ROPE_THETA ** (-jnp.arange(0, QK_ROPE, 2, dtype=F32) / QK_ROPE)
    ang = jnp.arange(s, dtype=F32)[:, None] * inv[None, :]
    c, sn = jnp.cos(ang), jnp.sin(ang)
    return jnp.concatenate([c, c, sn, sn], axis=1)


def _encoder(x, w):
    bsz, s, d = x.shape
    q, k, v, u = _in_stage(x, _rope_table(s), w)
    att = _attention(q, k, v)
    yf, yb = _s5_scan(u, w)
    n = bsz * s
    x1, xn, qp = _mid_stage(x.reshape(n, d), att.reshape(n, -1), yf.reshape(n, -1),
                            yb.reshape(n, -1), u.reshape(n, -1), w)
    n1, e1, r2, e2 = _route(qp, w['sub_keys'])
    y = _peer(xn, x1, n1, e1, r2, e2, w)
    return y.reshape(bsz, s, d)


def kernel(x_prompt, x_sample, mix_norm, w_in, q_norm, kv_norm, w_uq, w_ukv, lam_re, lam_im, log_dt, b_re, b_im, c_re, c_im, d_skip, w_glu, b_glu, w_out, ffn_norm, w_query, sub_keys, expert_u, expert_v, final_norm):
    w = _prepare(mix_norm, w_in, q_norm, kv_norm, w_uq, w_ukv, lam_re, lam_im, log_dt, b_re, b_im,
                 c_re, c_im, d_skip, w_glu, b_glu, w_out, ffn_norm, w_query, sub_keys, expert_u,
                 expert_v, final_norm)
    return (_encoder(x_prompt, w), _encoder(x_sample, w))
```

```python
import functools
import math

import jax
import jax.numpy as jnp
from jax import lax
from jax.experimental import pallas as pl
from jax.experimental.pallas import tpu as pltpu

F32 = jnp.float32
BF16 = jnp.bfloat16

EPS = 1e-6
ROPE_THETA = 10000.0
LANES = 128
SUBLANES = 8
VMEM_LIMIT = 56 * 1024 * 1024

N_HEADS = 4
QK_NOPE = 128
QK_ROPE = 64
V_HEAD = 128
QK_PAD = 256
Q_LORA = 384
KV_LORA = 256
SSM_WIDTH = 512
SSM_GROUPS = 32
SSM_GROUP_CH = 16
SSM_STATE = 64
SSM_CHUNKS = SSM_GROUPS * SSM_STATE // LANES
PEER_HEADS = 8
PEER_NKEYS = 128
PEER_TOPK = 16
GELU_C = math.sqrt(2.0 / math.pi)


def _gelu(x):
    return 0.5 * x * (1.0 + jnp.tanh(GELU_C * (x + 0.044715 * (x * x * x))))


def _rms(x, g):
    return x * lax.rsqrt(jnp.mean(x * x, axis=-1, keepdims=True) + EPS) * g


def _params(*sem):
    return pltpu.CompilerParams(dimension_semantics=sem, vmem_limit_bytes=VMEM_LIMIT)


def _in_kernel(x_ref, cs_ref, g_ref, win_ref, qn_ref, kvn_ref, wq_ref, wkv_ref,
               q_ref, k_ref, v_ref, u_ref):
    x = x_ref[...]
    h = _rms(x, g_ref[...])
    proj = jnp.dot(h.astype(BF16), win_ref[...], preferred_element_type=F32)
    o1 = Q_LORA
    o2 = o1 + KV_LORA
    o3 = o2 + 2 * QK_ROPE
    cq = proj[:, :o1]
    ckv = proj[:, o1:o2]
    kp = proj[:, o2:o3]
    u_ref[...] = proj[:, o3:]
    cs = cs_ref[...]
    lane = lax.broadcasted_iota(jnp.int32, kp.shape, 1)
    t = kp * cs
    kpe = jnp.where(lane < QK_ROPE, t + pltpu.roll(t, QK_ROPE, axis=1), 0.0)
    scale = (QK_NOPE + QK_ROPE) ** -0.5
    q = jnp.dot(_rms(cq, qn_ref[...]).astype(BF16), wq_ref[...], preferred_element_type=F32) * scale
    kv = jnp.dot(_rms(ckv, kvn_ref[...]).astype(BF16), wkv_ref[...], preferred_element_type=F32)
    for hd in range(N_HEADS):
        b = hd * QK_PAD
        tq = q[:, b + QK_NOPE:b + QK_PAD] * cs
        q_ref[hd, :, :QK_NOPE] = q[:, b:b + QK_NOPE].astype(BF16)
        q_ref[hd, :, QK_NOPE:] = (tq + pltpu.roll(tq, QK_ROPE, axis=1)).astype(BF16)
        k_ref[hd, :, :QK_NOPE] = kv[:, b:b + QK_NOPE].astype(BF16)
        k_ref[hd, :, QK_NOPE:] = kpe.astype(BF16)
        v_ref[hd] = kv[:, b + QK_NOPE:b + QK_PAD].astype(BF16)


def _in_stage(x, cs, w, tt=512):
    bsz, s, d = x.shape
    tt = min(tt, s)
    grid = (bsz, s // tt)
    full = lambda a: pl.BlockSpec(a.shape, lambda b, i: (0,) * a.ndim)
    hs = lambda wd: pl.BlockSpec((None, N_HEADS, tt, wd), lambda b, i: (b, 0, i, 0))
    return pl.pallas_call(
        _in_kernel,
        grid=grid,
        in_specs=[pl.BlockSpec((None, tt, d), lambda b, i: (b, i, 0)),
                  pl.BlockSpec((tt, 2 * QK_ROPE), lambda b, i: (i, 0)),
                  full(w['mix_norm']), full(w['w_in']), full(w['q_norm']), full(w['kv_norm']),
                  full(w['w_q']), full(w['w_kv'])],
        out_specs=[hs(QK_PAD), hs(QK_PAD), hs(V_HEAD),
                   pl.BlockSpec((None, tt, SSM_WIDTH), lambda b, i: (b, i, 0))],
        out_shape=[jax.ShapeDtypeStruct((bsz, N_HEADS, s, QK_PAD), BF16),
                   jax.ShapeDtypeStruct((bsz, N_HEADS, s, QK_PAD), BF16),
                   jax.ShapeDtypeStruct((bsz, N_HEADS, s, V_HEAD), BF16),
                   jax.ShapeDtypeStruct((bsz, s, SSM_WIDTH), F32)],
        compiler_params=_params("parallel", "parallel"),
        name="in_stage",
    )(x, cs, w['mix_norm'], w['w_in'], w['q_norm'], w['kv_norm'], w['w_q'], w['w_kv'])


def _flash_kernel(q_ref, k_ref, v_ref, o_ref, m_ref, acc_ref, s0, s1, p0, p1, al0, al1, *, tk):
    n = k_ref.shape[0] // tk
    m_ref[...] = jnp.full(m_ref.shape, -jnp.inf, F32)
    acc_ref[...] = jnp.zeros(acc_ref.shape, F32)
    ones = jnp.ones((tk, LANES), BF16)
    s_buf, p_buf, al_buf = (s0, s1), (p0, p1), (al0, al1)

    def scores(j, slot):
        kj = k_ref[pl.ds(pl.multiple_of(j * tk, tk), tk), :]
        s_buf[slot][...] = lax.dot_general(q_ref[...], kj, (((1,), (1,)), ((), ())),
                                           preferred_element_type=F32)

    def exps(slot):
        s = s_buf[slot][...]
        m_prev = m_ref[...]
        m_new = jnp.maximum(m_prev, jnp.max(s, axis=1, keepdims=True))
        p_buf[slot][...] = jnp.exp(s - jnp.concatenate([m_new] * (tk // LANES), axis=1)).astype(BF16)
        al_buf[slot][...] = jnp.exp(m_prev - m_new)
        m_ref[...] = m_new

    def values(j, slot):
        vj = jnp.concatenate([v_ref[pl.ds(pl.multiple_of(j * tk, tk), tk), :], ones], axis=1)
        acc_ref[...] = (jnp.concatenate([al_buf[slot][...]] * 2, axis=1) * acc_ref[...]
                        + jnp.dot(p_buf[slot][...], vj, preferred_element_type=F32))

    scores(0, 0)
    scores(1, 1)
    exps(0)

    def body(jj, carry):
        j = 2 * jj + 2
        scores(j, 0)
        exps(1)
        values(j - 2, 0)
        scores(j + 1, 1)
        exps(0)
        values(j - 1, 1)
        return carry

    lax.fori_loop(0, (n - 2) // 2, body, 0, unroll=True)
    exps(1)
    values(n - 2, 0)
    values(n - 1, 1)
    o_ref[...] = (acc_ref[:, :V_HEAD] / acc_ref[:, V_HEAD:]).astype(o_ref.dtype)


def _attention(q, k, v, tq=1024, tk=2048):
    bsz, nh, s, _ = q.shape
    tq = min(tq, s)
    tk = min(tk, s // 2)
    assert s % (2 * tk) == 0 and s % tq == 0
    return pl.pallas_call(
        functools.partial(_flash_kernel, tk=tk),
        grid=(bsz, nh, s // tq),
        in_specs=[pl.BlockSpec((None, None, tq, QK_PAD), lambda b, h, i: (b, h, i, 0)),
                  pl.BlockSpec((None, None, s, QK_PAD), lambda b, h, i: (b, h, 0, 0)),
                  pl.BlockSpec((None, None, s, V_HEAD), lambda b, h, i: (b, h, 0, 0))],
        out_specs=pl.BlockSpec((None, tq, V_HEAD), lambda b, h, i: (b, i, h)),
        out_shape=jax.ShapeDtypeStruct((bsz, s, nh * V_HEAD), BF16),
        scratch_shapes=[pltpu.VMEM((tq, LANES), F32), pltpu.VMEM((tq, 2 * V_HEAD), F32),
                        pltpu.VMEM((tq, tk), F32), pltpu.VMEM((tq, tk), F32),
                        pltpu.VMEM((tq, tk), BF16), pltpu.VMEM((tq, tk), BF16),
                        pltpu.VMEM((tq, LANES), F32), pltpu.VMEM((tq, LANES), F32)],
        compiler_params=_params("parallel", "parallel", "arbitrary"),
        name="attention",
    )(q, k, v)


def _s5_kernel(uf_ref, ub_ref, wf_ref, wb_ref, af_ref, ab_ref, cf_ref, cb_ref,
               yf_ref, yb_ref, fre, fim, bre, bim, sfre, sfim, sbre, sbim, st_ref, *, tc, pitch):
    half = SSM_CHUNKS // 2 * pitch

    @pl.when(pl.program_id(1) == 0)
    def _():
        st_ref[...] = jnp.zeros(st_ref.shape, F32)

    for u_ref, w_ref, re, im in ((uf_ref, wf_ref, fre, fim), (ub_ref, wb_ref, bre, bim)):
        for blk in range(SSM_WIDTH // LANES):
            ub = u_ref[:, blk * LANES:(blk + 1) * LANES].astype(BF16)
            for cc in range(SSM_CHUNKS * LANES // SSM_WIDTH):
                ch = blk * (SSM_CHUNKS * LANES // SSM_WIDTH) + cc
                r = jnp.dot(ub, w_ref[ch], preferred_element_type=F32)
                re[pl.ds(ch * pitch, tc), :] = r[:, :LANES]
                im[pl.ds(ch * pitch, tc), :] = r[:, LANES:]

    af = af_ref[...]
    ab = ab_ref[...]

    def step(a, st, re, im, sre, sim, t):
        out = []
        for hf in range(2):
            rows = pl.ds(hf * half + t, SUBLANES, stride=pitch)
            a_re, a_im = a[hf], a[2 + hf]
            s_re, s_im = st[hf], st[2 + hf]
            n_re = a_re * s_re - a_im * s_im + re[rows, :]
            n_im = a_re * s_im + a_im * s_re + im[rows, :]
            sre[rows, :] = n_re
            sim[rows, :] = n_im
            out.append((n_re, n_im))
        return (out[0][0], out[1][0], out[0][1], out[1][1])

    def body(t, carry):
        sf, sb = carry
        sf = step(af, sf, fre, fim, sfre, sfim, t)
        sb = step(ab, sb, bre, bim, sbre, sbim, tc - 1 - t)
        return (sf, sb)

    st = st_ref[...]
    init = (tuple(st[0, i] for i in range(4)), tuple(st[1, i] for i in range(4)))
    sf, sb = lax.fori_loop(0, tc, body, init, unroll=8)
    for i in range(4):
        st_ref[0, i] = sf[i]
        st_ref[1, i] = sb[i]

    for y_ref, c_ref, re, im in ((yf_ref, cf_ref, sfre, sfim), (yb_ref, cb_ref, sbre, sbim)):
        for blk in range(SSM_WIDTH // LANES):
            acc = None
            for cc in range(SSM_CHUNKS * LANES // SSM_WIDTH):
                ch = blk * (SSM_CHUNKS * LANES // SSM_WIDTH) + cc
                s = jnp.concatenate([re[pl.ds(ch * pitch, tc), :].astype(BF16),
                                     im[pl.ds(ch * pitch, tc), :].astype(BF16)], axis=1)
                d = jnp.dot(s, c_ref[ch], preferred_element_type=F32)
                acc = d if acc is None else acc + d
            y_ref[:, blk * LANES:(blk + 1) * LANES] = acc


def _s5_scan(u, w, tc=512):
    bsz, s, _ = u.shape
    tc = min(tc, s)
    n = s // tc
    pitch = tc + SUBLANES // 2
    rows = SSM_CHUNKS * pitch
    full = lambda a: pl.BlockSpec(a.shape, lambda b, c: (0,) * a.ndim)
    ublk = lambda f: pl.BlockSpec((None, tc, SSM_WIDTH), f)
    fwd = lambda b, c: (b, c, 0)
    bwd = lambda b, c: (b, n - 1 - c, 0)
    return pl.pallas_call(
        functools.partial(_s5_kernel, tc=tc, pitch=pitch),
        grid=(bsz, n),
        in_specs=[ublk(fwd), ublk(bwd), full(w['s5_wf']), full(w['s5_wb']),
                  full(w['s5_af']), full(w['s5_ab']), full(w['s5_cf']), full(w['s5_cb'])],
        out_specs=[ublk(fwd), ublk(bwd)],
        out_shape=[jax.ShapeDtypeStruct(u.shape, F32), jax.ShapeDtypeStruct(u.shape, F32)],
        scratch_shapes=[pltpu.VMEM((rows, LANES), F32) for _ in range(8)]
                       + [pltpu.VMEM((2, 4, SUBLANES, LANES), F32)],
        compiler_params=_params("parallel", "arbitrary"),
        name="s5_scan",
    )(u, u, w['s5_wf'], w['s5_wb'], w['s5_af'], w['s5_ab'], w['s5_cf'], w['s5_cb'])


def _mid_kernel(x_ref, att_ref, yf_ref, yb_ref, u_ref, dsk_ref, wglu_ref, bglu_ref,
                woa_ref, wos_ref, fn_ref, wqry_ref, x1_ref, xn_ref, qp_ref):
    y = dsk_ref[...] * u_ref[...] + yf_ref[...] + yb_ref[...]
    z = _gelu(y)
    gate = jnp.dot(z.astype(BF16), wglu_ref[...], preferred_element_type=F32) + bglu_ref[...]
    z = z * (1.0 / (1.0 + jnp.exp(-gate)))
    mix = (jnp.dot(att_ref[...], woa_ref[...], preferred_element_type=F32)
           + jnp.dot(z.astype(BF16), wos_ref[...], preferred_element_type=F32))
    x1 = x_ref[...] + mix
    x1_ref[...] = x1
    xn = _rms(x1, fn_ref[...]).astype(BF16)
    xn_ref[...] = xn
    qp_ref[...] = jnp.dot(xn, wqry_ref[...], preferred_element_type=F32).astype(BF16)


def _mid_stage(x, att, yf, yb, u, w, tt=512):
    n, d = x.shape
    tt = min(tt, n)
    full = lambda a: pl.BlockSpec(a.shape, lambda i: (0,) * a.ndim)
    row = lambda wd: pl.BlockSpec((tt, wd), lambda i: (i, 0))
    nq = w['w_query'].shape[1]
    return pl.pallas_call(
        _mid_kernel,
        grid=(n // tt,),
        in_specs=[row(d), row(att.shape[1]), row(SSM_WIDTH), row(SSM_WIDTH), row(SSM_WIDTH),
                  full(w['d_skip']), full(w['w_glu']), full(w['b_glu']), full(w['w_out_att']),
                  full(w['w_out_ssm']), full(w['ffn_norm']), full(w['w_query'])],
        out_specs=[row(d), row(d), row(nq)],
        out_shape=[jax.ShapeDtypeStruct((n, d), F32), jax.ShapeDtypeStruct((n, d), BF16),
                   jax.ShapeDtypeStruct((n, nq), BF16)],
        compiler_params=_params("parallel"),
        name="mid_stage",
    )(x, att, yf, yb, u, w['d_skip'], w['w_glu'], w['b_glu'], w['w_out_att'], w['w_out_ssm'],
      w['ffn_norm'], w['w_query'])


def _merge_desc(x):
    n = len(x)
    if n == 1:
        return x
    h = n // 2
    hi = [jnp.maximum(x[i], x[i + h]) for i in range(h)]
    lo = [jnp.minimum(x[i], x[i + h]) for i in range(h)]
    return _merge_desc(hi) + _merge_desc(lo)


def _sort_desc(x):
    n = len(x)
    if n == 1:
        return x
    h = n // 2
    return _merge_desc(_sort_desc(x[:h]) + _sort_desc(x[h:])[::-1])


def _merge_top(a, b):
    k = len(a)
    c = [jnp.maximum(a[i], b[k - 1 - i]) if k - 1 - i < len(b) else a[i] for i in range(k)]
    return _merge_desc(c)


def _route_kernel(qp_ref, keys_ref, n1_ref, e1_ref, r2_ref, e2_ref, sc_ref, out_ref, *, pitch):
    k_top = PEER_TOPK
    nk = PEER_NKEYS
    nsub = qp_ref.shape[0] // LANES

    for side in range(2):
        kmat = keys_ref[side]
        for j in range(nsub):
            qj = qp_ref[j * LANES:(j + 1) * LANES, side * LANES:(side + 1) * LANES]
            sc_ref[side, pl.ds(j * pitch, nk), :] = lax.dot_general(
                kmat, qj, (((1,), (1,)), ((), ())), preferred_element_type=F32)

    def key_rows(kk):
        return pl.ds(kk, nsub, stride=pitch)

    def top_sorted(side):
        best = None
        for g in range(nk // k_top):
            grp = _sort_desc([sc_ref[side, key_rows(g * k_top + i), :] for i in range(k_top)])
            best = grp if best is None else _merge_top(best, grp)
        return best

    v1 = top_sorted(0)
    v2 = top_sorted(1)

    lens = [k_top // (a + 1) for a in range(k_top)]
    cand = [[v1[a] + v2[b] for b in range(lens[a])] for a in range(k_top)]
    top = cand[0]
    a = 1
    while lens[a] > 1:
        top = _merge_top(top, cand[a])
        a += 1
    top = _merge_top(top, [cand[i][0] for i in range(a, k_top)])
    thr = top[k_top - 1]

    m = cand[0][0]
    z = None
    for row in cand:
        for c in row:
            e = jnp.where(c >= thr, jnp.exp(c - m), 0.0)
            z = e if z is None else z + e
    inv_z = 1.0 / z

    n_big = 3
    big = []
    for a in range(n_big):
        na = jnp.zeros_like(thr)
        for c in cand[a]:
            na = na + jnp.where(c >= thr, 1.0, 0.0)
        big.append(na)

    inf = jnp.full(thr.shape, jnp.inf, F32)
    tb = []
    for b in range(lens[n_big]):
        t = inf
        for a in range(k_top):
            if b < lens[a]:
                t = jnp.minimum(t, jnp.where(cand[a][b] >= thr, v1[a], inf))
        tb.append(t)

    def bf16_bits(x):
        return pltpu.bitcast(x.astype(BF16).astype(F32), jnp.uint32)

    def pack2(lo, hi):
        return pltpu.bitcast((bf16_bits(lo) >> 16) | bf16_bits(hi), F32)

    for kk in range(nk):
        rows = key_rows(kk)
        s1 = sc_ref[0, rows, :]
        cnt = jnp.zeros_like(s1)
        for b in range(lens[n_big]):
            cnt = jnp.where(s1 >= tb[b], float(b + 1), cnt)
        for a in range(n_big - 1, -1, -1):
            cnt = jnp.where(s1 == v1[a], big[a], cnt)
        out_ref[0, rows, :] = pack2(cnt, cnt)
        g1 = jnp.exp(s1 - v1[0]) * inv_z
        out_ref[1, rows, :] = pack2(g1, g1)

    def second(kk):
        s2 = sc_ref[1, key_rows(kk), :]
        rank = jnp.full(s2.shape, float(k_top), F32)
        for b in range(k_top - 1, -1, -1):
            rank = jnp.where(s2 >= v2[b], float(b), rank)
        return rank, jnp.exp(s2 - v2[0])

    for kp in range(nk // 2):
        r_lo, g_lo = second(2 * kp)
        r_hi, g_hi = second(2 * kp + 1)
        out_ref[2, key_rows(kp), :] = pack2(r_lo, r_hi)
        out_ref[3, key_rows(kp), :] = pack2(g_lo, g_hi)

    for j in range(nsub):
        cols = slice(j * LANES, (j + 1) * LANES)
        n1_ref[:, cols] = out_ref[0, pl.ds(j * pitch, nk), :]
        e1_ref[:, cols] = out_ref[1, pl.ds(j * pitch, nk), :]
        r2_ref[:, cols] = out_ref[2, pl.ds(j * pitch, nk // 2), :]
        e2_ref[:, cols] = out_ref[3, pl.ds(j * pitch, nk // 2), :]


def _route(qp, keys):
    n = qp.shape[0]
    tt = min(SUBLANES * LANES, n)
    nsub = tt // LANES
    pitch = PEER_NKEYS + SUBLANES
    rows = nsub * pitch
    hk = 2 * LANES
    nk, nk2 = PEER_NKEYS, PEER_NKEYS // 2
    blk = lambda r: pl.BlockSpec((None, r, tt), lambda i, h: (h, 0, i))
    shp = lambda r: jax.ShapeDtypeStruct((PEER_HEADS, r, n), F32)
    return pl.pallas_call(
        functools.partial(_route_kernel, pitch=pitch),
        grid=(n // tt, PEER_HEADS),
        in_specs=[pl.BlockSpec((tt, hk), lambda i, h: (i, h)),
                  pl.BlockSpec((2, None, PEER_NKEYS, LANES), lambda i, h: (0, h, 0, 0))],
        out_specs=[blk(nk), blk(nk), blk(nk2), blk(nk2)],
        out_shape=[shp(nk), shp(nk), shp(nk2), shp(nk2)],
        scratch_shapes=[pltpu.VMEM((2, rows, LANES), F32), pltpu.VMEM((4, rows, LANES), F32)],
        compiler_params=_params("parallel", "parallel"),
        name="peer_route",
    )(qp, keys)


def _peer_kernel(xn_ref, x1_ref, n1_ref, e1_ref, r2_ref, e2_ref, u_ref, vt_ref, fn_ref,
                 y_ref, a_ref, hid_ref, acc_ref, *, tcol):
    e = pl.program_id(1)
    tt = xn_ref.shape[0]
    rows_per_tile = u_ref.shape[0] // PEER_NKEYS

    @pl.when(e == 0)
    def _():
        acc_ref[...] = jnp.zeros(acc_ref.shape, F32)

    a_ref[...] = lax.dot_general(u_ref[...], xn_ref[...], (((1,), (1,)), ((), ())),
                                 preferred_element_type=F32)

    def bcast_row(ref, h, row, cols):
        w = jnp.broadcast_to(ref[h, row:row + 1, cols], (SUBLANES, tcol))
        return jnp.concatenate([pltpu.bitcast(w, BF16)] * (PEER_NKEYS // (2 * SUBLANES)), axis=0)

    def keys_tile(ref, h, cols):
        return pltpu.bitcast(ref[h, :, cols], BF16)

    for i1 in range(rows_per_tile):
        for c in range(tt // tcol):
            cols = slice(c * tcol, (c + 1) * tcol)
            g = None
            for h in range(PEER_HEADS):
                sel = (jnp.where(keys_tile(r2_ref, h, cols) < bcast_row(n1_ref, h, i1, cols),
                                 keys_tile(e2_ref, h, cols), jnp.zeros((), BF16))
                       * bcast_row(e1_ref, h, i1, cols))
                g = sel if g is None else g + sel
            a = a_ref[i1 * PEER_NKEYS:(i1 + 1) * PEER_NKEYS, cols].astype(BF16)
            hid_ref[i1 * PEER_NKEYS:(i1 + 1) * PEER_NKEYS, cols] = _gelu(a) * g

    acc_ref[...] += jnp.dot(vt_ref[...], hid_ref[...], preferred_element_type=F32)

    @pl.when(e == pl.num_programs(1) - 1)
    def _():
        x2 = x1_ref[...] + acc_ref[...].T
        y_ref[...] = _rms(x2, fn_ref[...])


def _peer(xn, x1, n1, e1, r2, e2, w, tt=1024, te=1024, tcol=1024):
    n, d = xn.shape
    tt = min(tt, n)
    tcol = min(tcol, tt)
    ne = w['expert_u'].shape[0]
    rt = lambda: pl.BlockSpec((PEER_HEADS, PEER_NKEYS // 2, tt), lambda i, e: (0, 0, i))
    r1 = lambda: pl.BlockSpec((PEER_HEADS, te // PEER_NKEYS, tt), lambda i, e: (0, e, i))
    return pl.pallas_call(
        functools.partial(_peer_kernel, tcol=tcol),
        grid=(n // tt, ne // te),
        in_specs=[pl.BlockSpec((tt, d), lambda i, e: (i, 0)),
                  pl.BlockSpec((tt, d), lambda i, e: (i, 0)),
                  r1(), r1(), rt(), rt(),
                  pl.BlockSpec((te, d), lambda i, e: (e, 0)),
                  pl.BlockSpec((d, te), lambda i, e: (0, e)),
                  pl.BlockSpec((1, d), lambda i, e: (0, 0))],
        out_specs=pl.BlockSpec((tt, d), lambda i, e: (i, 0)),
        out_shape=jax.ShapeDtypeStruct((n, d), F32),
        scratch_shapes=[pltpu.VMEM((te, tt), F32), pltpu.VMEM((te, tt), BF16),
                        pltpu.VMEM((d, tt), F32)],
        compiler_params=_params("parallel", "arbitrary"),
        name="peer_experts",
    )(xn, x1, n1, e1, r2, e2, w['expert_u'], w['expert_vt'], w['final_norm'])


def _rot_cols(wm):
    hw = wm.shape[-1] // 2
    return jnp.concatenate([-wm[..., hw:], wm[..., :hw]], axis=-1)


def _s5_direction_params(lam_re, lam_im, log_dt, b_re, b_im, c_re, c_im):
    g, p, ch = SSM_GROUPS, SSM_STATE, SSM_GROUP_CH
    dt = jnp.exp(log_dt)[:, None]
    mag = jnp.exp(lam_re * dt)
    a_re = mag * jnp.cos(lam_im * dt)
    a_im = mag * jnp.sin(lam_im * dt)
    nr = a_re - 1.0
    den = lam_re * lam_re + lam_im * lam_im
    f_re = (nr * lam_re + a_im * lam_im) / den
    f_im = (a_im * lam_re - nr * lam_im) / den
    bb_re = f_re[..., None] * b_re - f_im[..., None] * b_im
    bb_im = f_re[..., None] * b_im + f_im[..., None] * b_re
    eye = jnp.eye(g, dtype=F32)
    per_blk = SSM_CHUNKS * LANES // SSM_WIDTH

    def expand(bb):
        nb = SSM_WIDTH // LANES
        m = jnp.einsum('gph,gk->ghkp', bb, eye).reshape(nb, LANES, nb, per_blk, LANES)
        return jnp.einsum('arack->acrk', m).reshape(SSM_CHUNKS, LANES, LANES)

    def readout(cm):
        nb = SSM_WIDTH // LANES
        m = jnp.einsum('ghp,gk->gpkh', cm, eye).reshape(nb, per_blk, LANES, nb, LANES)
        return jnp.einsum('acrak->acrk', m).reshape(SSM_CHUNKS, LANES, LANES)

    wexp = jnp.concatenate([expand(bb_re), expand(bb_im)], axis=-1).astype(BF16)
    cmat = jnp.concatenate([readout(c_re), readout(-c_im)], axis=1).astype(BF16)
    a = jnp.concatenate([a_re.reshape(2, SUBLANES, LANES), a_im.reshape(2, SUBLANES, LANES)])
    return wexp, a, cmat


def _prepare(mix_norm, w_in, q_norm, kv_norm, w_uq, w_ukv, lam_re, lam_im, log_dt, b_re, b_im,
             c_re, c_im, d_skip, w_glu, b_glu, w_out, ffn_norm, w_query, sub_keys, expert_u,
             expert_v, final_norm):
    o1 = Q_LORA
    o2 = o1 + KV_LORA
    o3 = o2 + QK_ROPE
    wi = w_in[0]
    w_kpe = wi[:, o2:o3]
    w = {}
    w['mix_norm'] = mix_norm[0][None, :]
    w['w_in'] = jnp.concatenate([wi[:, :o2], w_kpe, _rot_cols(w_kpe), wi[:, o3:]], axis=1).astype(BF16)
    w['q_norm'] = q_norm[0][None, :]
    w['kv_norm'] = kv_norm[0][None, :]
    wq = w_uq[0]
    wq_pe = wq[..., QK_NOPE:]
    w['w_q'] = jnp.concatenate([wq[..., :QK_NOPE], wq_pe, _rot_cols(wq_pe)], axis=-1).reshape(
        Q_LORA, N_HEADS * QK_PAD).astype(BF16)
    w['w_kv'] = w_ukv[0].reshape(KV_LORA, N_HEADS * (QK_NOPE + V_HEAD)).astype(BF16)
    for name, d in (('f', 0), ('b', 1)):
        wexp, a, cmat = _s5_direction_params(lam_re[0, d], lam_im[0, d], log_dt[0, d], b_re[0, d],
                                             b_im[0, d], c_re[0, d], c_im[0, d])
        w['s5_w' + name], w['s5_a' + name], w['s5_c' + name] = wexp, a, cmat
    w['d_skip'] = d_skip[0].reshape(1, SSM_WIDTH)
    w['w_glu'] = w_glu[0].astype(BF16)
    w['b_glu'] = b_glu[0][None, :]
    mla_w = N_HEADS * V_HEAD
    w['w_out_att'] = w_out[0][:mla_w].astype(BF16)
    w['w_out_ssm'] = w_out[0][mla_w:].astype(BF16)
    w['ffn_norm'] = ffn_norm[0][None, :]
    w['w_query'] = w_query[0].astype(BF16)
    w['sub_keys'] = sub_keys[0].astype(BF16)
    w['expert_u'] = expert_u[0].astype(BF16)
    w['expert_vt'] = expert_v[0].astype(BF16).T
    w['final_norm'] = final_norm[None, :]
    return w


def _rope_table(s):
    inv = ROPE_THETA ** (-jnp.arange(0, QK_ROPE, 2, dtype=F32) / QK_ROPE)
    ang = jnp.arange(s, dtype=F32)[:, None] * inv[None, :]
    c, sn = jnp.cos(ang), jnp.sin(ang)
    return jnp.concatenate([c, c, sn, sn], axis=1)


def _encoder(x, w):
    bsz, s, d = x.shape
    q, k, v, u = _in_stage(x, _rope_table(s), w)
    att = _attention(q, k, v)
    yf, yb = _s5_scan(u, w)
    n = bsz * s
    x1, xn, qp = _mid_stage(x.reshape(n, d), att.reshape(n, -1), yf.reshape(n, -1),
                            yb.reshape(n, -1), u.reshape(n, -1), w)
    n1, e1, r2, e2 = _route(qp, w['sub_keys'])
    y = _peer(xn, x1, n1, e1, r2, e2, w)
    return y.reshape(bsz, s, d)


def kernel(x_prompt, x_sample, mix_norm, w_in, q_norm, kv_norm, w_uq, w_ukv, lam_re, lam_im, log_dt, b_re, b_im, c_re, c_im, d_skip, w_glu, b_glu, w_out, ffn_norm, w_query, sub_keys, expert_u, expert_v, final_norm):
    w = _prepare(mix_norm, w_in, q_norm, kv_norm, w_uq, w_ukv, lam_re, lam_im, log_dt, b_re, b_im,
                 c_re, c_im, d_skip, w_glu, b_glu, w_out, ffn_norm, w_query, sub_keys, expert_u,
                 expert_v, final_norm)
    return (_encoder(x_prompt, w), _encoder(x_sample, w))
```

```python
import functools
import math

import jax
import jax.numpy as jnp
from jax import lax
from jax.experimental import pallas as pl
from jax.experimental.pallas import tpu as pltpu

F32 = jnp.float32
BF16 = jnp.bfloat16

EPS = 1e-6
ROPE_THETA = 10000.0
LANES = 128
SUBLANES = 8
VMEM_LIMIT = 56 * 1024 * 1024

N_HEADS = 4
QK_NOPE = 128
QK_ROPE = 64
V_HEAD = 128
QK_PAD = 256
Q_LORA = 384
KV_LORA = 256
SSM_WIDTH = 512
SSM_GROUPS = 32
SSM_GROUP_CH = 16
SSM_STATE = 64
SSM_CHUNKS = SSM_GROUPS * SSM_STATE // LANES
PEER_HEADS = 8
PEER_NKEYS = 128
PEER_TOPK = 16
GELU_C = math.sqrt(2.0 / math.pi)


def _gelu(x):
    return 0.5 * x * (1.0 + jnp.tanh(GELU_C * (x + 0.044715 * (x * x * x))))


def _rms(x, g):
    return x * lax.rsqrt(jnp.mean(x * x, axis=-1, keepdims=True) + EPS) * g


def _params(*sem):
    return pltpu.CompilerParams(dimension_semantics=sem, vmem_limit_bytes=VMEM_LIMIT)


def _in_kernel(x_ref, cs_ref, g_ref, win_ref, qn_ref, kvn_ref, wq_ref, wkv_ref,
               q_ref, k_ref, v_ref, u_ref):
    x = x_ref[...]
    h = _rms(x, g_ref[...])
    proj = jnp.dot(h.astype(BF16), win_ref[...], preferred_element_type=F32)
    o1 = Q_LORA
    o2 = o1 + KV_LORA
    o3 = o2 + 2 * QK_ROPE
    cq = proj[:, :o1]
    ckv = proj[:, o1:o2]
    kp = proj[:, o2:o3]
    u_ref[...] = proj[:, o3:]
    cs = cs_ref[...]
    lane = lax.broadcasted_iota(jnp.int32, kp.shape, 1)
    t = kp * cs
    kpe = jnp.where(lane < QK_ROPE, t + pltpu.roll(t, QK_ROPE, axis=1), 0.0)
    scale = (QK_NOPE + QK_ROPE) ** -0.5
    q = jnp.dot(_rms(cq, qn_ref[...]).astype(BF16), wq_ref[...], preferred_element_type=F32) * scale
    kv = jnp.dot(_rms(ckv, kvn_ref[...]).astype(BF16), wkv_ref[...], preferred_element_type=F32)
    for hd in range(N_HEADS):
        b = hd * QK_PAD
        tq = q[:, b + QK_NOPE:b + QK_PAD] * cs
        q_ref[hd, :, :QK_NOPE] = q[:, b:b + QK_NOPE].astype(BF16)
        q_ref[hd, :, QK_NOPE:] = (tq + pltpu.roll(tq, QK_ROPE, axis=1)).astype(BF16)
        k_ref[hd, :, :QK_NOPE] = kv[:, b:b + QK_NOPE].astype(BF16)
        k_ref[hd, :, QK_NOPE:] = kpe.astype(BF16)
        v_ref[hd] = kv[:, b + QK_NOPE:b + QK_PAD].astype(BF16)


def _in_stage(x, cs, w, tt=512):
    bsz, s, d = x.shape
    tt = min(tt, s)
    grid = (bsz, s // tt)
    full = lambda a: pl.BlockSpec(a.shape, lambda b, i: (0,) * a.ndim)
    hs = lambda wd: pl.BlockSpec((None, N_HEADS, tt, wd), lambda b, i: (b, 0, i, 0))
    return pl.pallas_call(
        _in_kernel,
        grid=grid,
        in_specs=[pl.BlockSpec((None, tt, d), lambda b, i: (b, i, 0)),
                  pl.BlockSpec((tt, 2 * QK_ROPE), lambda b, i: (i, 0)),
                  full(w['mix_norm']), full(w['w_in']), full(w['q_norm']), full(w['kv_norm']),
                  full(w['w_q']), full(w['w_kv'])],
        out_specs=[hs(QK_PAD), hs(QK_PAD), hs(V_HEAD),
                   pl.BlockSpec((None, tt, SSM_WIDTH), lambda b, i: (b, i, 0))],
        out_shape=[jax.ShapeDtypeStruct((bsz, N_HEADS, s, QK_PAD), BF16),
                   jax.ShapeDtypeStruct((bsz, N_HEADS, s, QK_PAD), BF16),
                   jax.ShapeDtypeStruct((bsz, N_HEADS, s, V_HEAD), BF16),
                   jax.ShapeDtypeStruct((bsz, s, SSM_WIDTH), F32)],
        compiler_params=_params("parallel", "parallel"),
        name="in_stage",
    )(x, cs, w['mix_norm'], w['w_in'], w['q_norm'], w['kv_norm'], w['w_q'], w['w_kv'])


def _flash_kernel(q_ref, k_ref, v_ref, o_ref, m_ref, acc_ref, s0, s1, p0, p1, al0, al1, *, tk):
    n = k_ref.shape[0] // tk
    m_ref[...] = jnp.full(m_ref.shape, -jnp.inf, F32)
    acc_ref[...] = jnp.zeros(acc_ref.shape, F32)
    ones = jnp.ones((tk, LANES), BF16)
    s_buf, p_buf, al_buf = (s0, s1), (p0, p1), (al0, al1)

    def scores(j, slot):
        kj = k_ref[pl.ds(pl.multiple_of(j * tk, tk), tk), :]
        s_buf[slot][...] = lax.dot_general(q_ref[...], kj, (((1,), (1,)), ((), ())),
                                           preferred_element_type=F32)

    def exps(slot):
        s = s_buf[slot][...]
        m_prev = m_ref[...]
        m_new = jnp.maximum(m_prev, jnp.max(s, axis=1, keepdims=True))
        p_buf[slot][...] = jnp.exp(s - jnp.concatenate([m_new] * (tk // LANES), axis=1)).astype(BF16)
        al_buf[slot][...] = jnp.exp(m_prev - m_new)
        m_ref[...] = m_new

    def values(j, slot):
        vj = jnp.concatenate([v_ref[pl.ds(pl.multiple_of(j * tk, tk), tk), :], ones], axis=1)
        acc_ref[...] = (jnp.concatenate([al_buf[slot][...]] * 2, axis=1) * acc_ref[...]
                        + jnp.dot(p_buf[slot][...], vj, preferred_element_type=F32))

    scores(0, 0)
    scores(1, 1)
    exps(0)

    def body(jj, carry):
        j = 2 * jj + 2
        scores(j, 0)
        exps(1)
        values(j - 2, 0)
        scores(j + 1, 1)
        exps(0)
        values(j - 1, 1)
        return carry

    lax.fori_loop(0, (n - 2) // 2, body, 0, unroll=True)
    exps(1)
    values(n - 2, 0)
    values(n - 1, 1)
    o_ref[...] = (acc_ref[:, :V_HEAD] / acc_ref[:, V_HEAD:]).astype(o_ref.dtype)


def _attention(q, k, v, tq=2048, tk=1024):
    bsz, nh, s, _ = q.shape
    tq = min(tq, s)
    tk = min(tk, s // 2)
    assert s % (2 * tk) == 0 and s % tq == 0
    return pl.pallas_call(
        functools.partial(_flash_kernel, tk=tk),
        grid=(bsz, nh, s // tq),
        in_specs=[pl.BlockSpec((None, None, tq, QK_PAD), lambda b, h, i: (b, h, i, 0)),
                  pl.BlockSpec((None, None, s, QK_PAD), lambda b, h, i: (b, h, 0, 0)),
                  pl.BlockSpec((None, None, s, V_HEAD), lambda b, h, i: (b, h, 0, 0))],
        out_specs=pl.BlockSpec((None, tq, V_HEAD), lambda b, h, i: (b, i, h)),
        out_shape=jax.ShapeDtypeStruct((bsz, s, nh * V_HEAD), BF16),
        scratch_shapes=[pltpu.VMEM((tq, LANES), F32), pltpu.VMEM((tq, 2 * V_HEAD), F32),
                        pltpu.VMEM((tq, tk), F32), pltpu.VMEM((tq, tk), F32),
                        pltpu.VMEM((tq, tk), BF16), pltpu.VMEM((tq, tk), BF16),
                        pltpu.VMEM((tq, LANES), F32), pltpu.VMEM((tq, LANES), F32)],
        compiler_params=_params("parallel", "parallel", "arbitrary"),
        name="attention",
    )(q, k, v)


def _s5_kernel(uf_ref, ub_ref, wf_ref, wb_ref, af_ref, ab_ref, cf_ref, cb_ref,
               yf_ref, yb_ref, fre, fim, bre, bim, sfre, sfim, sbre, sbim, st_ref, *, tc, pitch):
    half = SSM_CHUNKS // 2 * pitch

    @pl.when(pl.program_id(1) == 0)
    def _():
        st_ref[...] = jnp.zeros(st_ref.shape, F32)

    for u_ref, w_ref, re, im in ((uf_ref, wf_ref, fre, fim), (ub_ref, wb_ref, bre, bim)):
        for blk in range(SSM_WIDTH // LANES):
            ub = u_ref[:, blk * LANES:(blk + 1) * LANES].astype(BF16)
            for cc in range(SSM_CHUNKS * LANES // SSM_WIDTH):
                ch = blk * (SSM_CHUNKS * LANES // SSM_WIDTH) + cc
                r = jnp.dot(ub, w_ref[ch], preferred_element_type=F32)
                re[pl.ds(ch * pitch, tc), :] = r[:, :LANES]
                im[pl.ds(ch * pitch, tc), :] = r[:, LANES:]

    af = af_ref[...]
    ab = ab_ref[...]

    def step(a, st, re, im, sre, sim, t):
        out = []
        for hf in range(2):
            rows = pl.ds(hf * half + t, SUBLANES, stride=pitch)
            a_re, a_im = a[hf], a[2 + hf]
            s_re, s_im = st[hf], st[2 + hf]
            n_re = a_re * s_re - a_im * s_im + re[rows, :]
            n_im = a_re * s_im + a_im * s_re + im[rows, :]
            sre[rows, :] = n_re
            sim[rows, :] = n_im
            out.append((n_re, n_im))
        return (out[0][0], out[1][0], out[0][1], out[1][1])

    def body(t, carry):
        sf, sb = carry
        sf = step(af, sf, fre, fim, sfre, sfim, t)
        sb = step(ab, sb, bre, bim, sbre, sbim, tc - 1 - t)
        return (sf, sb)

    st = st_ref[...]
    init = (tuple(st[0, i] for i in range(4)), tuple(st[1, i] for i in range(4)))
    sf, sb = lax.fori_loop(0, tc, body, init, unroll=8)
    for i in range(4):
        st_ref[0, i] = sf[i]
        st_ref[1, i] = sb[i]

    for y_ref, c_ref, re, im in ((yf_ref, cf_ref, sfre, sfim), (yb_ref, cb_ref, sbre, sbim)):
        for blk in range(SSM_WIDTH // LANES):
            acc = None
            for cc in range(SSM_CHUNKS * LANES // SSM_WIDTH):
                ch = blk * (SSM_CHUNKS * LANES // SSM_WIDTH) + cc
                s = jnp.concatenate([re[pl.ds(ch * pitch, tc), :].astype(BF16),
                                     im[pl.ds(ch * pitch, tc), :].astype(BF16)], axis=1)
                d = jnp.dot(s, c_ref[ch], preferred_element_type=F32)
                acc = d if acc is None else acc + d
            y_ref[:, blk * LANES:(blk + 1) * LANES] = acc


def _s5_scan(u, w, tc=512):
    bsz, s, _ = u.shape
    tc = min(tc, s)
    n = s // tc
    pitch = tc + SUBLANES // 2
    rows = SSM_CHUNKS * pitch
    full = lambda a: pl.BlockSpec(a.shape, lambda b, c: (0,) * a.ndim)
    ublk = lambda f: pl.BlockSpec((None, tc, SSM_WIDTH), f)
    fwd = lambda b, c: (b, c, 0)
    bwd = lambda b, c: (b, n - 1 - c, 0)
    return pl.pallas_call(
        functools.partial(_s5_kernel, tc=tc, pitch=pitch),
        grid=(bsz, n),
        in_specs=[ublk(fwd), ublk(bwd), full(w['s5_wf']), full(w['s5_wb']),
                  full(w['s5_af']), full(w['s5_ab']), full(w['s5_cf']), full(w['s5_cb'])],
        out_specs=[ublk(fwd), ublk(bwd)],
        out_shape=[jax.ShapeDtypeStruct(u.shape, F32), jax.ShapeDtypeStruct(u.shape, F32)],
        scratch_shapes=[pltpu.VMEM((rows, LANES), F32) for _ in range(8)]
                       + [pltpu.VMEM((2, 4, SUBLANES, LANES), F32)],
        compiler_params=_params("parallel", "arbitrary"),
        name="s5_scan",
    )(u, u, w['s5_wf'], w['s5_wb'], w['s5_af'], w['s5_ab'], w['s5_cf'], w['s5_cb'])


def _mid_kernel(x_ref, att_ref, yf_ref, yb_ref, u_ref, dsk_ref, wglu_ref, bglu_ref,
                woa_ref, wos_ref, fn_ref, wqry_ref, x1_ref, xn_ref, qp_ref):
    y = dsk_ref[...] * u_ref[...] + yf_ref[...] + yb_ref[...]
    z = _gelu(y)
    gate = jnp.dot(z.astype(BF16), wglu_ref[...], preferred_element_type=F32) + bglu_ref[...]
    z = z * (1.0 / (1.0 + jnp.exp(-gate)))
    mix = (jnp.dot(att_ref[...], woa_ref[...], preferred_element_type=F32)
           + jnp.dot(z.astype(BF16), wos_ref[...], preferred_element_type=F32))
    x1 = x_ref[...] + mix
    x1_ref[...] = x1
    xn = _rms(x1, fn_ref[...]).astype(BF16)
    xn_ref[...] = xn
    qp_ref[...] = jnp.dot(xn, wqry_ref[...], preferred_element_type=F32).astype(BF16)


def _mid_stage(x, att, yf, yb, u, w, tt=512):
    n, d = x.shape
    tt = min(tt, n)
    full = lambda a: pl.BlockSpec(a.shape, lambda i: (0,) * a.ndim)
    row = lambda wd: pl.BlockSpec((tt, wd), lambda i: (i, 0))
    nq = w['w_query'].shape[1]
    return pl.pallas_call(
        _mid_kernel,
        grid=(n // tt,),
        in_specs=[row(d), row(att.shape[1]), row(SSM_WIDTH), row(SSM_WIDTH), row(SSM_WIDTH),
                  full(w['d_skip']), full(w['w_glu']), full(w['b_glu']), full(w['w_out_att']),
                  full(w['w_out_ssm']), full(w['ffn_norm']), full(w['w_query'])],
        out_specs=[row(d), row(d), row(nq)],
        out_shape=[jax.ShapeDtypeStruct((n, d), F32), jax.ShapeDtypeStruct((n, d), BF16),
                   jax.ShapeDtypeStruct((n, nq), BF16)],
        compiler_params=_params("parallel"),
        name="mid_stage",
    )(x, att, yf, yb, u, w['d_skip'], w['w_glu'], w['b_glu'], w['w_out_att'], w['w_out_ssm'],
      w['ffn_norm'], w['w_query'])


def _merge_desc(x):
    n = len(x)
    if n == 1:
        return x
    h = n // 2
    hi = [jnp.maximum(x[i], x[i + h]) for i in range(h)]
    lo = [jnp.minimum(x[i], x[i + h]) for i in range(h)]
    return _merge_desc(hi) + _merge_desc(lo)


def _sort_desc(x):
    n = len(x)
    if n == 1:
        return x
    h = n // 2
    return _merge_desc(_sort_desc(x[:h]) + _sort_desc(x[h:])[::-1])


def _merge_top(a, b):
    k = len(a)
    c = [jnp.maximum(a[i], b[k - 1 - i]) if k - 1 - i < len(b) else a[i] for i in range(k)]
    return _merge_desc(c)


def _route_kernel(qp_ref, keys_ref, n1_ref, e1_ref, r2_ref, e2_ref, sc_ref, out_ref, *, pitch):
    k_top = PEER_TOPK
    nk = PEER_NKEYS
    nsub = qp_ref.shape[0] // LANES

    for side in range(2):
        kmat = keys_ref[side]
        for j in range(nsub):
            qj = qp_ref[j * LANES:(j + 1) * LANES, side * LANES:(side + 1) * LANES]
            sc_ref[side, pl.ds(j * pitch, nk), :] = lax.dot_general(
                kmat, qj, (((1,), (1,)), ((), ())), preferred_element_type=F32)

    def key_rows(kk):
        return pl.ds(kk, nsub, stride=pitch)

    def top_sorted(side):
        best = None
        for g in range(nk // k_top):
            grp = _sort_desc([sc_ref[side, key_rows(g * k_top + i), :] for i in range(k_top)])
            best = grp if best is None else _merge_top(best, grp)
        return best

    v1 = top_sorted(0)
    v2 = top_sorted(1)

    lens = [k_top // (a + 1) for a in range(k_top)]
    cand = [[v1[a] + v2[b] for b in range(lens[a])] for a in range(k_top)]
    top = cand[0]
    a = 1
    while lens[a] > 1:
        top = _merge_top(top, cand[a])
        a += 1
    top = _merge_top(top, [cand[i][0] for i in range(a, k_top)])
    thr = top[k_top - 1]

    m = cand[0][0]
    z = None
    for row in cand:
        for c in row:
            e = jnp.where(c >= thr, jnp.exp(c - m), 0.0)
            z = e if z is None else z + e
    inv_z = 1.0 / z

    n_big = 3
    big = []
    for a in range(n_big):
        na = jnp.zeros_like(thr)
        for c in cand[a]:
            na = na + jnp.where(c >= thr, 1.0, 0.0)
        big.append(na)

    inf = jnp.full(thr.shape, jnp.inf, F32)
    tb = []
    for b in range(lens[n_big]):
        t = inf
        for a in range(k_top):
            if b < lens[a]:
                t = jnp.minimum(t, jnp.where(cand[a][b] >= thr, v1[a], inf))
        tb.append(t)

    def bf16_bits(x):
        return pltpu.bitcast(x.astype(BF16).astype(F32), jnp.uint32)

    def pack2(lo, hi):
        return pltpu.bitcast((bf16_bits(lo) >> 16) | bf16_bits(hi), F32)

    for kk in range(nk):
        rows = key_rows(kk)
        s1 = sc_ref[0, rows, :]
        cnt = jnp.zeros_like(s1)
        for b in range(lens[n_big]):
            cnt = jnp.where(s1 >= tb[b], float(b + 1), cnt)
        for a in range(n_big - 1, -1, -1):
            cnt = jnp.where(s1 == v1[a], big[a], cnt)
        out_ref[0, rows, :] = pack2(cnt, cnt)
        g1 = jnp.exp(s1 - v1[0]) * inv_z
        out_ref[1, rows, :] = pack2(g1, g1)

    def second(kk):
        s2 = sc_ref[1, key_rows(kk), :]
        rank = jnp.full(s2.shape, float(k_top), F32)
        for b in range(k_top - 1, -1, -1):
            rank = jnp.where(s2 >= v2[b], float(b), rank)
        return rank, jnp.exp(s2 - v2[0])

    for kp in range(nk // 2):
        r_lo, g_lo = second(2 * kp)
        r_hi, g_hi = second(2 * kp + 1)
        out_ref[2, key_rows(kp), :] = pack2(r_lo, r_hi)
        out_ref[3, key_rows(kp), :] = pack2(g_lo, g_hi)

    for j in range(nsub):
        cols = slice(j * LANES, (j + 1) * LANES)
        n1_ref[:, cols] = out_ref[0, pl.ds(j * pitch, nk), :]
        e1_ref[:, cols] = out_ref[1, pl.ds(j * pitch, nk), :]
        r2_ref[:, cols] = out_ref[2, pl.ds(j * pitch, nk // 2), :]
        e2_ref[:, cols] = out_ref[3, pl.ds(j * pitch, nk // 2), :]


def _route(qp, keys):
    n = qp.shape[0]
    tt = min(SUBLANES * LANES, n)
    nsub = tt // LANES
    pitch = PEER_NKEYS + SUBLANES
    rows = nsub * pitch
    hk = 2 * LANES
    nk, nk2 = PEER_NKEYS, PEER_NKEYS // 2
    blk = lambda r: pl.BlockSpec((None, r, tt), lambda i, h: (h, 0, i))
    shp = lambda r: jax.ShapeDtypeStruct((PEER_HEADS, r, n), F32)
    return pl.pallas_call(
        functools.partial(_route_kernel, pitch=pitch),
        grid=(n // tt, PEER_HEADS),
        in_specs=[pl.BlockSpec((tt, hk), lambda i, h: (i, h)),
                  pl.BlockSpec((2, None, PEER_NKEYS, LANES), lambda i, h: (0, h, 0, 0))],
        out_specs=[blk(nk), blk(nk), blk(nk2), blk(nk2)],
        out_shape=[shp(nk), shp(nk), shp(nk2), shp(nk2)],
        scratch_shapes=[pltpu.VMEM((2, rows, LANES), F32), pltpu.VMEM((4, rows, LANES), F32)],
        compiler_params=_params("parallel", "parallel"),
        name="peer_route",
    )(qp, keys)


def _peer_kernel(xn_ref, x1_ref, n1_ref, e1_ref, r2_ref, e2_ref, u_ref, vt_ref, fn_ref,
                 y_ref, a_ref, hid_ref, acc_ref, *, tcol):
    e = pl.program_id(1)
    tt = xn_ref.shape[0]
    rows_per_tile = u_ref.shape[0] // PEER_NKEYS

    @pl.when(e == 0)
    def _():
        acc_ref[...] = jnp.zeros(acc_ref.shape, F32)

    a_ref[...] = lax.dot_general(u_ref[...], xn_ref[...], (((1,), (1,)), ((), ())),
                                 preferred_element_type=F32)

    def bcast_row(ref, h, row, cols):
        w = jnp.broadcast_to(ref[h, row:row + 1, cols], (SUBLANES, tcol))
        return jnp.concatenate([pltpu.bitcast(w, BF16)] * (PEER_NKEYS // (2 * SUBLANES)), axis=0)

    def keys_tile(ref, h, cols):
        return pltpu.bitcast(ref[h, :, cols], BF16)

    for i1 in range(rows_per_tile):
        for c in range(tt // tcol):
            cols = slice(c * tcol, (c + 1) * tcol)
            g = None
            for h in range(PEER_HEADS):
                sel = (jnp.where(keys_tile(r2_ref, h, cols) < bcast_row(n1_ref, h, i1, cols),
                                 keys_tile(e2_ref, h, cols), jnp.zeros((), BF16))
                       * bcast_row(e1_ref, h, i1, cols))
                g = sel if g is None else g + sel
            a = a_ref[i1 * PEER_NKEYS:(i1 + 1) * PEER_NKEYS, cols].astype(BF16)
            hid_ref[i1 * PEER_NKEYS:(i1 + 1) * PEER_NKEYS, cols] = _gelu(a) * g

    acc_ref[...] += jnp.dot(vt_ref[...], hid_ref[...], preferred_element_type=F32)

    @pl.when(e == pl.num_programs(1) - 1)
    def _():
        x2 = x1_ref[...] + acc_ref[...].T
        y_ref[...] = _rms(x2, fn_ref[...])


def _peer(xn, x1, n1, e1, r2, e2, w, tt=1024, te=1024, tcol=1024):
    n, d = xn.shape
    tt = min(tt, n)
    tcol = min(tcol, tt)
    ne = w['expert_u'].shape[0]
    rt = lambda: pl.BlockSpec((PEER_HEADS, PEER_NKEYS // 2, tt), lambda i, e: (0, 0, i))
    r1 = lambda: pl.BlockSpec((PEER_HEADS, te // PEER_NKEYS, tt), lambda i, e: (0, e, i))
    return pl.pallas_call(
        functools.partial(_peer_kernel, tcol=tcol),
        grid=(n // tt, ne // te),
        in_specs=[pl.BlockSpec((tt, d), lambda i, e: (i, 0)),
                  pl.BlockSpec((tt, d), lambda i, e: (i, 0)),
                  r1(), r1(), rt(), rt(),
                  pl.BlockSpec((te, d), lambda i, e: (e, 0)),
                  pl.BlockSpec((d, te), lambda i, e: (0, e)),
                  pl.BlockSpec((1, d), lambda i, e: (0, 0))],
        out_specs=pl.BlockSpec((tt, d), lambda i, e: (i, 0)),
        out_shape=jax.ShapeDtypeStruct((n, d), F32),
        scratch_shapes=[pltpu.VMEM((te, tt), F32), pltpu.VMEM((te, tt), BF16),
                        pltpu.VMEM((d, tt), F32)],
        compiler_params=_params("parallel", "arbitrary"),
        name="peer_experts",
    )(xn, x1, n1, e1, r2, e2, w['expert_u'], w['expert_vt'], w['final_norm'])


def _rot_cols(wm):
    hw = wm.shape[-1] // 2
    return jnp.concatenate([-wm[..., hw:], wm[..., :hw]], axis=-1)


def _s5_direction_params(lam_re, lam_im, log_dt, b_re, b_im, c_re, c_im):
    g, p, ch = SSM_GROUPS, SSM_STATE, SSM_GROUP_CH
    dt = jnp.exp(log_dt)[:, None]
    mag = jnp.exp(lam_re * dt)
    a_re = mag * jnp.cos(lam_im * dt)
    a_im = mag * jnp.sin(lam_im * dt)
    nr = a_re - 1.0
    den = lam_re * lam_re + lam_im * lam_im
    f_re = (nr * lam_re + a_im * lam_im) / den
    f_im = (a_im * lam_re - nr * lam_im) / den
    bb_re = f_re[..., None] * b_re - f_im[..., None] * b_im
    bb_im = f_re[..., None] * b_im + f_im[..., None] * b_re
    eye = jnp.eye(g, dtype=F32)
    per_blk = SSM_CHUNKS * LANES // SSM_WIDTH

    def expand(bb):
        nb = SSM_WIDTH // LANES
        m = jnp.einsum('gph,gk->ghkp', bb, eye).reshape(nb, LANES, nb, per_blk, LANES)
        return jnp.einsum('arack->acrk', m).reshape(SSM_CHUNKS, LANES, LANES)

    def readout(cm):
        nb = SSM_WIDTH // LANES
        m = jnp.einsum('ghp,gk->gpkh', cm, eye).reshape(nb, per_blk, LANES, nb, LANES)
        return jnp.einsum('acrak->acrk', m).reshape(SSM_CHUNKS, LANES, LANES)

    wexp = jnp.concatenate([expand(bb_re), expand(bb_im)], axis=-1).astype(BF16)
    cmat = jnp.concatenate([readout(c_re), readout(-c_im)], axis=1).astype(BF16)
    a = jnp.concatenate([a_re.reshape(2, SUBLANES, LANES), a_im.reshape(2, SUBLANES, LANES)])
    return wexp, a, cmat


def _prepare(mix_norm, w_in, q_norm, kv_norm, w_uq, w_ukv, lam_re, lam_im, log_dt, b_re, b_im,
             c_re, c_im, d_skip, w_glu, b_glu, w_out, ffn_norm, w_query, sub_keys, expert_u,
             expert_v, final_norm):
    o1 = Q_LORA
    o2 = o1 + KV_LORA
    o3 = o2 + QK_ROPE
    wi = w_in[0]
    w_kpe = wi[:, o2:o3]
    w = {}
    w['mix_norm'] = mix_norm[0][None, :]
    w['w_in'] = jnp.concatenate([wi[:, :o2], w_kpe, _rot_cols(w_kpe), wi[:, o3:]], axis=1).astype(BF16)
    w['q_norm'] = q_norm[0][None, :]
    w['kv_norm'] = kv_norm[0][None, :]
    wq = w_uq[0]
    wq_pe = wq[..., QK_NOPE:]
    w['w_q'] = jnp.concatenate([wq[..., :QK_NOPE], wq_pe, _rot_cols(wq_pe)], axis=-1).reshape(
        Q_LORA, N_HEADS * QK_PAD).astype(BF16)
    w['w_kv'] = w_ukv[0].reshape(KV_LORA, N_HEADS * (QK_NOPE + V_HEAD)).astype(BF16)
    for name, d in (('f', 0), ('b', 1)):
        wexp, a, cmat = _s5_direction_params(lam_re[0, d], lam_im[0, d], log_dt[0, d], b_re[0, d],
                                             b_im[0, d], c_re[0, d], c_im[0, d])
        w['s5_w' + name], w['s5_a' + name], w['s5_c' + name] = wexp, a, cmat
    w['d_skip'] = d_skip[0].reshape(1, SSM_WIDTH)
    w['w_glu'] = w_glu[0].astype(BF16)
    w['b_glu'] = b_glu[0][None, :]
    mla_w = N_HEADS * V_HEAD
    w['w_out_att'] = w_out[0][:mla_w].astype(BF16)
    w['w_out_ssm'] = w_out[0][mla_w:].astype(BF16)
    w['ffn_norm'] = ffn_norm[0][None, :]
    w['w_query'] = w_query[0].astype(BF16)
    w['sub_keys'] = sub_keys[0].astype(BF16)
    w['expert_u'] = expert_u[0].astype(BF16)
    w['expert_vt'] = expert_v[0].astype(BF16).T
    w['final_norm'] = final_norm[None, :]
    return w


def _rope_table(s):
    inv = ROPE_THETA ** (-jnp.arange(0, QK_ROPE, 2, dtype=F32) / QK_ROPE)
    ang = jnp.arange(s, dtype=F32)[:, None] * inv[None, :]
    c, sn = jnp.cos(ang), jnp.sin(ang)
    return jnp.concatenate([c, c, sn, sn], axis=1)


def _encoder(x, w):
    bsz, s, d = x.shape
    q, k, v, u = _in_stage(x, _rope_table(s), w)
    att = _attention(q, k, v)
    yf, yb = _s5_scan(u, w)
    n = bsz * s
    x1, xn, qp = _mid_stage(x.reshape(n, d), att.reshape(n, -1), yf.reshape(n, -1),
                            yb.reshape(n, -1), u.reshape(n, -1), w)
    n1, e1, r2, e2 = _route(qp, w['sub_keys'])
    y = _peer(xn, x1, n1, e1, r2, e2, w)
    return y.reshape(bsz, s, d)


def kernel(x_prompt, x_sample, mix_norm, w_in, q_norm, kv_norm, w_uq, w_ukv, lam_re, lam_im, log_dt, b_re, b_im, c_re, c_im, d_skip, w_glu, b_glu, w_out, ffn_norm, w_query, sub_keys, expert_u, expert_v, final_norm):
    w = _prepare(mix_norm, w_in, q_norm, kv_norm, w_uq, w_ukv, lam_re, lam_im, log_dt, b_re, b_im,
                 c_re, c_im, d_skip, w_glu, b_glu, w_out, ffn_norm, w_query, sub_keys, expert_u,
                 expert_v, final_norm)
    return (_encoder(x_prompt, w), _encoder(x_sample, w))
```

```python
import functools
import math

import jax
import jax.numpy as jnp
from jax import lax
from jax.experimental import pallas as pl
from jax.experimental.pallas import tpu as pltpu

F32 = jnp.float32
BF16 = jnp.bfloat16

EPS = 1e-6
ROPE_THETA = 10000.0
LANES = 128
SUBLANES = 8
VMEM_LIMIT = 56 * 1024 * 1024

N_HEADS = 4
QK_NOPE = 128
QK_ROPE = 64
V_HEAD = 128
QK_PAD = 256
Q_LORA = 384
KV_LORA = 256
SSM_WIDTH = 512
SSM_GROUPS = 32
SSM_GROUP_CH = 16
SSM_STATE = 64
SSM_CHUNKS = SSM_GROUPS * SSM_STATE // LANES
PEER_HEADS = 8
PEER_NKEYS = 128
PEER_TOPK = 16
GELU_C = math.sqrt(2.0 / math.pi)


def _gelu(x):
    return 0.5 * x * (1.0 + jnp.tanh(GELU_C * (x + 0.044715 * (x * x * x))))


def _rms(x, g):
    return x * lax.rsqrt(jnp.mean(x * x, axis=-1, keepdims=True) + EPS) * g


def _params(*sem):
    return pltpu.CompilerParams(dimension_semantics=sem, vmem_limit_bytes=VMEM_LIMIT)


def _in_kernel(x_ref, cs_ref, g_ref, win_ref, qn_ref, kvn_ref, wq_ref, wkv_ref,
               q_ref, k_ref, v_ref, u_ref):
    x = x_ref[...]
    h = _rms(x, g_ref[...])
    proj = jnp.dot(h.astype(BF16), win_ref[...], preferred_element_type=F32)
    o1 = Q_LORA
    o2 = o1 + KV_LORA
    o3 = o2 + 2 * QK_ROPE
    cq = proj[:, :o1]
    ckv = proj[:, o1:o2]
    kp = proj[:, o2:o3]
    u_ref[...] = proj[:, o3:]
    cs = cs_ref[...]
    lane = lax.broadcasted_iota(jnp.int32, kp.shape, 1)
    t = kp * cs
    kpe = jnp.where(lane < QK_ROPE, t + pltpu.roll(t, QK_ROPE, axis=1), 0.0)
    scale = (QK_NOPE + QK_ROPE) ** -0.5
    q = jnp.dot(_rms(cq, qn_ref[...]).astype(BF16), wq_ref[...], preferred_element_type=F32) * scale
    kv = jnp.dot(_rms(ckv, kvn_ref[...]).astype(BF16), wkv_ref[...], preferred_element_type=F32)
    for hd in range(N_HEADS):
        b = hd * QK_PAD
        tq = q[:, b + QK_NOPE:b + QK_PAD] * cs
        q_ref[hd, :, :QK_NOPE] = q[:, b:b + QK_NOPE].astype(BF16)
        q_ref[hd, :, QK_NOPE:] = (tq + pltpu.roll(tq, QK_ROPE, axis=1)).astype(BF16)
        k_ref[hd, :, :QK_NOPE] = kv[:, b:b + QK_NOPE].astype(BF16)
        k_ref[hd, :, QK_NOPE:] = kpe.astype(BF16)
        v_ref[hd] = kv[:, b + QK_NOPE:b + QK_PAD].astype(BF16)


def _in_stage(x, cs, w, tt=1024):
    bsz, s, d = x.shape
    tt = min(tt, s)
    grid = (bsz, s // tt)
    full = lambda a: pl.BlockSpec(a.shape, lambda b, i: (0,) * a.ndim)
    hs = lambda wd: pl.BlockSpec((None, N_HEADS, tt, wd), lambda b, i: (b, 0, i, 0))
    return pl.pallas_call(
        _in_kernel,
        grid=grid,
        in_specs=[pl.BlockSpec((None, tt, d), lambda b, i: (b, i, 0)),
                  pl.BlockSpec((tt, 2 * QK_ROPE), lambda b, i: (i, 0)),
                  full(w['mix_norm']), full(w['w_in']), full(w['q_norm']), full(w['kv_norm']),
                  full(w['w_q']), full(w['w_kv'])],
        out_specs=[hs(QK_PAD), hs(QK_PAD), hs(V_HEAD),
                   pl.BlockSpec((None, tt, SSM_WIDTH), lambda b, i: (b, i, 0))],
        out_shape=[jax.ShapeDtypeStruct((bsz, N_HEADS, s, QK_PAD), BF16),
                   jax.ShapeDtypeStruct((bsz, N_HEADS, s, QK_PAD), BF16),
                   jax.ShapeDtypeStruct((bsz, N_HEADS, s, V_HEAD), BF16),
                   jax.ShapeDtypeStruct((bsz, s, SSM_WIDTH), F32)],
        compiler_params=_params("parallel", "parallel"),
        name="in_stage",
    )(x, cs, w['mix_norm'], w['w_in'], w['q_norm'], w['kv_norm'], w['w_q'], w['w_kv'])


def _flash_kernel(q_ref, k_ref, v_ref, o_ref, m_ref, acc_ref, s0, s1, p0, p1, al0, al1, *, tk):
    n = k_ref.shape[0] // tk
    m_ref[...] = jnp.full(m_ref.shape, -jnp.inf, F32)
    acc_ref[...] = jnp.zeros(acc_ref.shape, F32)
    ones = jnp.ones((tk, LANES), BF16)
    s_buf, p_buf, al_buf = (s0, s1), (p0, p1), (al0, al1)

    def scores(j, slot):
        kj = k_ref[pl.ds(pl.multiple_of(j * tk, tk), tk), :]
        s_buf[slot][...] = lax.dot_general(q_ref[...], kj, (((1,), (1,)), ((), ())),
                                           preferred_element_type=F32)

    def exps(slot):
        s = s_buf[slot][...]
        m_prev = m_ref[...]
        m_new = jnp.maximum(m_prev, jnp.max(s, axis=1, keepdims=True))
        p_buf[slot][...] = jnp.exp(s - jnp.concatenate([m_new] * (tk // LANES), axis=1)).astype(BF16)
        al_buf[slot][...] = jnp.exp(m_prev - m_new)
        m_ref[...] = m_new

    def values(j, slot):
        vj = jnp.concatenate([v_ref[pl.ds(pl.multiple_of(j * tk, tk), tk), :], ones], axis=1)
        acc_ref[...] = (jnp.concatenate([al_buf[slot][...]] * 2, axis=1) * acc_ref[...]
                        + jnp.dot(p_buf[slot][...], vj, preferred_element_type=F32))

    scores(0, 0)
    scores(1, 1)
    exps(0)

    def body(jj, carry):
        j = 2 * jj + 2
        scores(j, 0)
        exps(1)
        values(j - 2, 0)
        scores(j + 1, 1)
        exps(0)
        values(j - 1, 1)
        return carry

    lax.fori_loop(0, (n - 2) // 2, body, 0, unroll=True)
    exps(1)
    values(n - 2, 0)
    values(n - 1, 1)
    o_ref[...] = (acc_ref[:, :V_HEAD] / acc_ref[:, V_HEAD:]).astype(o_ref.dtype)


def _attention(q, k, v, tq=512, tk=2048):
    bsz, nh, s, _ = q.shape
    tq = min(tq, s)
    tk = min(tk, s // 2)
    assert s % (2 * tk) == 0 and s % tq == 0
    return pl.pallas_call(
        functools.partial(_flash_kernel, tk=tk),
        grid=(bsz, nh, s // tq),
        in_specs=[pl.BlockSpec((None, None, tq, QK_PAD), lambda b, h, i: (b, h, i, 0)),
                  pl.BlockSpec((None, None, s, QK_PAD), lambda b, h, i: (b, h, 0, 0)),
                  pl.BlockSpec((None, None, s, V_HEAD), lambda b, h, i: (b, h, 0, 0))],
        out_specs=pl.BlockSpec((None, tq, V_HEAD), lambda b, h, i: (b, i, h)),
        out_shape=jax.ShapeDtypeStruct((bsz, s, nh * V_HEAD), BF16),
        scratch_shapes=[pltpu.VMEM((tq, LANES), F32), pltpu.VMEM((tq, 2 * V_HEAD), F32),
                        pltpu.VMEM((tq, tk), F32), pltpu.VMEM((tq, tk), F32),
                        pltpu.VMEM((tq, tk), BF16), pltpu.VMEM((tq, tk), BF16),
                        pltpu.VMEM((tq, LANES), F32), pltpu.VMEM((tq, LANES), F32)],
        compiler_params=_params("parallel", "parallel", "arbitrary"),
        name="attention",
    )(q, k, v)


def _s5_kernel(uf_ref, ub_ref, wf_ref, wb_ref, af_ref, ab_ref, cf_ref, cb_ref,
               yf_ref, yb_ref, fre, fim, bre, bim, sfre, sfim, sbre, sbim, st_ref, *, tc, pitch):
    half = SSM_CHUNKS // 2 * pitch

    @pl.when(pl.program_id(1) == 0)
    def _():
        st_ref[...] = jnp.zeros(st_ref.shape, F32)

    for u_ref, w_ref, re, im in ((uf_ref, wf_ref, fre, fim), (ub_ref, wb_ref, bre, bim)):
        for blk in range(SSM_WIDTH // LANES):
            ub = u_ref[:, blk * LANES:(blk + 1) * LANES].astype(BF16)
            for cc in range(SSM_CHUNKS * LANES // SSM_WIDTH):
                ch = blk * (SSM_CHUNKS * LANES // SSM_WIDTH) + cc
                r = jnp.dot(ub, w_ref[ch], preferred_element_type=F32)
                re[pl.ds(ch * pitch, tc), :] = r[:, :LANES]
                im[pl.ds(ch * pitch, tc), :] = r[:, LANES:]

    af = af_ref[...]
    ab = ab_ref[...]

    def step(a, st, re, im, sre, sim, t):
        out = []
        for hf in range(2):
            rows = pl.ds(hf * half + t, SUBLANES, stride=pitch)
            a_re, a_im = a[hf], a[2 + hf]
            s_re, s_im = st[hf], st[2 + hf]
            n_re = a_re * s_re - a_im * s_im + re[rows, :]
            n_im = a_re * s_im + a_im * s_re + im[rows, :]
            sre[rows, :] = n_re
            sim[rows, :] = n_im
            out.append((n_re, n_im))
        return (out[0][0], out[1][0], out[0][1], out[1][1])

    def body(t, carry):
        sf, sb = carry
        sf = step(af, sf, fre, fim, sfre, sfim, t)
        sb = step(ab, sb, bre, bim, sbre, sbim, tc - 1 - t)
        return (sf, sb)

    st = st_ref[...]
    init = (tuple(st[0, i] for i in range(4)), tuple(st[1, i] for i in range(4)))
    sf, sb = lax.fori_loop(0, tc, body, init, unroll=8)
    for i in range(4):
        st_ref[0, i] = sf[i]
        st_ref[1, i] = sb[i]

    for y_ref, c_ref, re, im in ((yf_ref, cf_ref, sfre, sfim), (yb_ref, cb_ref, sbre, sbim)):
        for blk in range(SSM_WIDTH // LANES):
            acc = None
            for cc in range(SSM_CHUNKS * LANES // SSM_WIDTH):
                ch = blk * (SSM_CHUNKS * LANES // SSM_WIDTH) + cc
                s = jnp.concatenate([re[pl.ds(ch * pitch, tc), :].astype(BF16),
                                     im[pl.ds(ch * pitch, tc), :].astype(BF16)], axis=1)
                d = jnp.dot(s, c_ref[ch], preferred_element_type=F32)
                acc = d if acc is None else acc + d
            y_ref[:, blk * LANES:(blk + 1) * LANES] = acc


def _s5_scan(u, w, tc=512):
    bsz, s, _ = u.shape
    tc = min(tc, s)
    n = s // tc
    pitch = tc + SUBLANES // 2
    rows = SSM_CHUNKS * pitch
    full = lambda a: pl.BlockSpec(a.shape, lambda b, c: (0,) * a.ndim)
    ublk = lambda f: pl.BlockSpec((None, tc, SSM_WIDTH), f)
    fwd = lambda b, c: (b, c, 0)
    bwd = lambda b, c: (b, n - 1 - c, 0)
    return pl.pallas_call(
        functools.partial(_s5_kernel, tc=tc, pitch=pitch),
        grid=(bsz, n),
        in_specs=[ublk(fwd), ublk(bwd), full(w['s5_wf']), full(w['s5_wb']),
                  full(w['s5_af']), full(w['s5_ab']), full(w['s5_cf']), full(w['s5_cb'])],
        out_specs=[ublk(fwd), ublk(bwd)],
        out_shape=[jax.ShapeDtypeStruct(u.shape, F32), jax.ShapeDtypeStruct(u.shape, F32)],
        scratch_shapes=[pltpu.VMEM((rows, LANES), F32) for _ in range(8)]
                       + [pltpu.VMEM((2, 4, SUBLANES, LANES), F32)],
        compiler_params=_params("parallel", "arbitrary"),
        name="s5_scan",
    )(u, u, w['s5_wf'], w['s5_wb'], w['s5_af'], w['s5_ab'], w['s5_cf'], w['s5_cb'])


def _mid_kernel(x_ref, att_ref, yf_ref, yb_ref, u_ref, dsk_ref, wglu_ref, bglu_ref,
                woa_ref, wos_ref, fn_ref, wqry_ref, x1_ref, xn_ref, qp_ref):
    y = dsk_ref[...] * u_ref[...] + yf_ref[...] + yb_ref[...]
    z = _gelu(y)
    gate = jnp.dot(z.astype(BF16), wglu_ref[...], preferred_element_type=F32) + bglu_ref[...]
    z = z * (1.0 / (1.0 + jnp.exp(-gate)))
    mix = (jnp.dot(att_ref[...], woa_ref[...], preferred_element_type=F32)
           + jnp.dot(z.astype(BF16), wos_ref[...], preferred_element_type=F32))
    x1 = x_ref[...] + mix
    x1_ref[...] = x1
    xn = _rms(x1, fn_ref[...]).astype(BF16)
    xn_ref[...] = xn
    qp_ref[...] = jnp.dot(xn, wqry_ref[...], preferred_element_type=F32).astype(BF16)


def _mid_stage(x, att, yf, yb, u, w, tt=1024):
    n, d = x.shape
    tt = min(tt, n)
    full = lambda a: pl.BlockSpec(a.shape, lambda i: (0,) * a.ndim)
    row = lambda wd: pl.BlockSpec((tt, wd), lambda i: (i, 0))
    nq = w['w_query'].shape[1]
    return pl.pallas_call(
        _mid_kernel,
        grid=(n // tt,),
        in_specs=[row(d), row(att.shape[1]), row(SSM_WIDTH), row(SSM_WIDTH), row(SSM_WIDTH),
                  full(w['d_skip']), full(w['w_glu']), full(w['b_glu']), full(w['w_out_att']),
                  full(w['w_out_ssm']), full(w['ffn_norm']), full(w['w_query'])],
        out_specs=[row(d), row(d), row(nq)],
        out_shape=[jax.ShapeDtypeStruct((n, d), F32), jax.ShapeDtypeStruct((n, d), BF16),
                   jax.ShapeDtypeStruct((n, nq), BF16)],
        compiler_params=_params("parallel"),
        name="mid_stage",
    )(x, att, yf, yb, u, w['d_skip'], w['w_glu'], w['b_glu'], w['w_out_att'], w['w_out_ssm'],
      w['ffn_norm'], w['w_query'])


def _merge_desc(x):
    n = len(x)
    if n == 1:
        return x
    h = n // 2
    hi = [jnp.maximum(x[i], x[i + h]) for i in range(h)]
    lo = [jnp.minimum(x[i], x[i + h]) for i in range(h)]
    return _merge_desc(hi) + _merge_desc(lo)


def _sort_desc(x):
    n = len(x)
    if n == 1:
        return x
    h = n // 2
    return _merge_desc(_sort_desc(x[:h]) + _sort_desc(x[h:])[::-1])


def _merge_top(a, b):
    k = len(a)
    c = [jnp.maximum(a[i], b[k - 1 - i]) if k - 1 - i < len(b) else a[i] for i in range(k)]
    return _merge_desc(c)


def _route_kernel(qp_ref, keys_ref, n1_ref, e1_ref, r2_ref, e2_ref, sc_ref, out_ref, *, pitch):
    k_top = PEER_TOPK
    nk = PEER_NKEYS
    nsub = qp_ref.shape[0] // LANES

    for side in range(2):
        kmat = keys_ref[side]
        for j in range(nsub):
            qj = qp_ref[j * LANES:(j + 1) * LANES, side * LANES:(side + 1) * LANES]
            sc_ref[side, pl.ds(j * pitch, nk), :] = lax.dot_general(
                kmat, qj, (((1,), (1,)), ((), ())), preferred_element_type=F32)

    def key_rows(kk):
        return pl.ds(kk, nsub, stride=pitch)

    def top_sorted(side):
        best = None
        for g in range(nk // k_top):
            grp = _sort_desc([sc_ref[side, key_rows(g * k_top + i), :] for i in range(k_top)])
            best = grp if best is None else _merge_top(best, grp)
        return best

    v1 = top_sorted(0)
    v2 = top_sorted(1)

    lens = [k_top // (a + 1) for a in range(k_top)]
    cand = [[v1[a] + v2[b] for b in range(lens[a])] for a in range(k_top)]
    top = cand[0]
    a = 1
    while lens[a] > 1:
        top = _merge_top(top, cand[a])
        a += 1
    top = _merge_top(top, [cand[i][0] for i in range(a, k_top)])
    thr = top[k_top - 1]

    m = cand[0][0]
    z = None
    for row in cand:
        for c in row:
            e = jnp.where(c >= thr, jnp.exp(c - m), 0.0)
            z = e if z is None else z + e
    inv_z = 1.0 / z

    n_big = 3
    big = []
    for a in range(n_big):
        na = jnp.zeros_like(thr)
        for c in cand[a]:
            na = na + jnp.where(c >= thr, 1.0, 0.0)
        big.append(na)

    inf = jnp.full(thr.shape, jnp.inf, F32)
    tb = []
    for b in range(lens[n_big]):
        t = inf
        for a in range(k_top):
            if b < lens[a]:
                t = jnp.minimum(t, jnp.where(cand[a][b] >= thr, v1[a], inf))
        tb.append(t)

    def bf16_bits(x):
        return pltpu.bitcast(x.astype(BF16).astype(F32), jnp.uint32)

    def pack2(lo, hi):
        return pltpu.bitcast((bf16_bits(lo) >> 16) | bf16_bits(hi), F32)

    for kk in range(nk):
        rows = key_rows(kk)
        s1 = sc_ref[0, rows, :]
        cnt = jnp.zeros_like(s1)
        for b in range(lens[n_big]):
            cnt = jnp.where(s1 >= tb[b], float(b + 1), cnt)
        for a in range(n_big - 1, -1, -1):
            cnt = jnp.where(s1 == v1[a], big[a], cnt)
        out_ref[0, rows, :] = pack2(cnt, cnt)
        g1 = jnp.exp(s1 - v1[0]) * inv_z
        out_ref[1, rows, :] = pack2(g1, g1)

    def second(kk):
        s2 = sc_ref[1, key_rows(kk), :]
        rank = jnp.full(s2.shape, float(k_top), F32)
        for b in range(k_top - 1, -1, -1):
            rank = jnp.where(s2 >= v2[b], float(b), rank)
        return rank, jnp.exp(s2 - v2[0])

    for kp in range(nk // 2):
        r_lo, g_lo = second(2 * kp)
        r_hi, g_hi = second(2 * kp + 1)
        out_ref[2, key_rows(kp), :] = pack2(r_lo, r_hi)
        out_ref[3, key_rows(kp), :] = pack2(g_lo, g_hi)

    for j in range(nsub):
        cols = slice(j * LANES, (j + 1) * LANES)
        n1_ref[:, cols] = out_ref[0, pl.ds(j * pitch, nk), :]
        e1_ref[:, cols] = out_ref[1, pl.ds(j * pitch, nk), :]
        r2_ref[:, cols] = out_ref[2, pl.ds(j * pitch, nk // 2), :]
        e2_ref[:, cols] = out_ref[3, pl.ds(j * pitch, nk // 2), :]


def _route(qp, keys):
    n = qp.shape[0]
    tt = min(SUBLANES * LANES, n)
    nsub = tt // LANES
    pitch = PEER_NKEYS + SUBLANES
    rows = nsub * pitch
    hk = 2 * LANES
    nk, nk2 = PEER_NKEYS, PEER_NKEYS // 2
    blk = lambda r: pl.BlockSpec((None, r, tt), lambda i, h: (h, 0, i))
    shp = lambda r: jax.ShapeDtypeStruct((PEER_HEADS, r, n), F32)
    return pl.pallas_call(
        functools.partial(_route_kernel, pitch=pitch),
        grid=(n // tt, PEER_HEADS),
        in_specs=[pl.BlockSpec((tt, hk), lambda i, h: (i, h)),
                  pl.BlockSpec((2, None, PEER_NKEYS, LANES), lambda i, h: (0, h, 0, 0))],
        out_specs=[blk(nk), blk(nk), blk(nk2), blk(nk2)],
        out_shape=[shp(nk), shp(nk), shp(nk2), shp(nk2)],
        scratch_shapes=[pltpu.VMEM((2, rows, LANES), F32), pltpu.VMEM((4, rows, LANES), F32)],
        compiler_params=_params("parallel", "parallel"),
        name="peer_route",
    )(qp, keys)


def _peer_kernel(xn_ref, x1_ref, n1_ref, e1_ref, r2_ref, e2_ref, u_ref, vt_ref, fn_ref,
                 y_ref, a_ref, hid_ref, acc_ref, *, tcol):
    e = pl.program_id(1)
    tt = xn_ref.shape[0]
    rows_per_tile = u_ref.shape[0] // PEER_NKEYS

    @pl.when(e == 0)
    def _():
        acc_ref[...] = jnp.zeros(acc_ref.shape, F32)

    a_ref[...] = lax.dot_general(u_ref[...], xn_ref[...], (((1,), (1,)), ((), ())),
                                 preferred_element_type=F32)

    def bcast_row(ref, h, row, cols):
        w = jnp.broadcast_to(ref[h, row:row + 1, cols], (SUBLANES, tcol))
        return jnp.concatenate([pltpu.bitcast(w, BF16)] * (PEER_NKEYS // (2 * SUBLANES)), axis=0)

    def keys_tile(ref, h, cols):
        return pltpu.bitcast(ref[h, :, cols], BF16)

    for i1 in range(rows_per_tile):
        for c in range(tt // tcol):
            cols = slice(c * tcol, (c + 1) * tcol)
            g = None
            for h in range(PEER_HEADS):
                sel = (jnp.where(keys_tile(r2_ref, h, cols) < bcast_row(n1_ref, h, i1, cols),
                                 keys_tile(e2_ref, h, cols), jnp.zeros((), BF16))
                       * bcast_row(e1_ref, h, i1, cols))
                g = sel if g is None else g + sel
            a = a_ref[i1 * PEER_NKEYS:(i1 + 1) * PEER_NKEYS, cols].astype(BF16)
            hid_ref[i1 * PEER_NKEYS:(i1 + 1) * PEER_NKEYS, cols] = _gelu(a) * g

    acc_ref[...] += jnp.dot(vt_ref[...], hid_ref[...], preferred_element_type=F32)

    @pl.when(e == pl.num_programs(1) - 1)
    def _():
        x2 = x1_ref[...] + acc_ref[...].T
        y_ref[...] = _rms(x2, fn_ref[...])


def _peer(xn, x1, n1, e1, r2, e2, w, tt=1024, te=1024, tcol=1024):
    n, d = xn.shape
    tt = min(tt, n)
    tcol = min(tcol, tt)
    ne = w['expert_u'].shape[0]
    rt = lambda: pl.BlockSpec((PEER_HEADS, PEER_NKEYS // 2, tt), lambda i, e: (0, 0, i))
    r1 = lambda: pl.BlockSpec((PEER_HEADS, te // PEER_NKEYS, tt), lambda i, e: (0, e, i))
    return pl.pallas_call(
        functools.partial(_peer_kernel, tcol=tcol),
        grid=(n // tt, ne // te),
        in_specs=[pl.BlockSpec((tt, d), lambda i, e: (i, 0)),
                  pl.BlockSpec((tt, d), lambda i, e: (i, 0)),
                  r1(), r1(), rt(), rt(),
                  pl.BlockSpec((te, d), lambda i, e: (e, 0)),
                  pl.BlockSpec((d, te), lambda i, e: (0, e)),
                  pl.BlockSpec((1, d), lambda i, e: (0, 0))],
        out_specs=pl.BlockSpec((tt, d), lambda i, e: (i, 0)),
        out_shape=jax.ShapeDtypeStruct((n, d), F32),
        scratch_shapes=[pltpu.VMEM((te, tt), F32), pltpu.VMEM((te, tt), BF16),
                        pltpu.VMEM((d, tt), F32)],
        compiler_params=_params("parallel", "arbitrary"),
        name="peer_experts",
    )(xn, x1, n1, e1, r2, e2, w['expert_u'], w['expert_vt'], w['final_norm'])


def _rot_cols(wm):
    hw = wm.shape[-1] // 2
    return jnp.concatenate([-wm[..., hw:], wm[..., :hw]], axis=-1)


def _s5_direction_params(lam_re, lam_im, log_dt, b_re, b_im, c_re, c_im):
    g, p, ch = SSM_GROUPS, SSM_STATE, SSM_GROUP_CH
    dt = jnp.exp(log_dt)[:, None]
    mag = jnp.exp(lam_re * dt)
    a_re = mag * jnp.cos(lam_im * dt)
    a_im = mag * jnp.sin(lam_im * dt)
    nr = a_re - 1.0
    den = lam_re * lam_re + lam_im * lam_im
    f_re = (nr * lam_re + a_im * lam_im) / den
    f_im = (a_im * lam_re - nr * lam_im) / den
    bb_re = f_re[..., None] * b_re - f_im[..., None] * b_im
    bb_im = f_re[..., None] * b_im + f_im[..., None] * b_re
    eye = jnp.eye(g, dtype=F32)
    per_blk = SSM_CHUNKS * LANES // SSM_WIDTH

    def expand(bb):
        nb = SSM_WIDTH // LANES
        m = jnp.einsum('gph,gk->ghkp', bb, eye).reshape(nb, LANES, nb, per_blk, LANES)
        return jnp.einsum('arack->acrk', m).reshape(SSM_CHUNKS, LANES, LANES)

    def readout(cm):
        nb = SSM_WIDTH // LANES
        m = jnp.einsum('ghp,gk->gpkh', cm, eye).reshape(nb, per_blk, LANES, nb, LANES)
        return jnp.einsum('acrak->acrk', m).reshape(SSM_CHUNKS, LANES, LANES)

    wexp = jnp.concatenate([expand(bb_re), expand(bb_im)], axis=-1).astype(BF16)
    cmat = jnp.concatenate([readout(c_re), readout(-c_im)], axis=1).astype(BF16)
    a = jnp.concatenate([a_re.reshape(2, SUBLANES, LANES), a_im.reshape(2, SUBLANES, LANES)])
    return wexp, a, cmat


def _prepare(mix_norm, w_in, q_norm, kv_norm, w_uq, w_ukv, lam_re, lam_im, log_dt, b_re, b_im,
             c_re, c_im, d_skip, w_glu, b_glu, w_out, ffn_norm, w_query, sub_keys, expert_u,
             expert_v, final_norm):
    o1 = Q_LORA
    o2 = o1 + KV_LORA
    o3 = o2 + QK_ROPE
    wi = w_in[0]
    w_kpe = wi[:, o2:o3]
    w = {}
    w['mix_norm'] = mix_norm[0][None, :]
    w['w_in'] = jnp.concatenate([wi[:, :o2], w_kpe, _rot_cols(w_kpe), wi[:, o3:]], axis=1).astype(BF16)
    w['q_norm'] = q_norm[0][None, :]
    w['kv_norm'] = kv_norm[0][None, :]
    wq = w_uq[0]
    wq_pe = wq[..., QK_NOPE:]
    w['w_q'] = jnp.concatenate([wq[..., :QK_NOPE], wq_pe, _rot_cols(wq_pe)], axis=-1).reshape(
        Q_LORA, N_HEADS * QK_PAD).astype(BF16)
    w['w_kv'] = w_ukv[0].reshape(KV_LORA, N_HEADS * (QK_NOPE + V_HEAD)).astype(BF16)
    for name, d in (('f', 0), ('b', 1)):
        wexp, a, cmat = _s5_direction_params(lam_re[0, d], lam_im[0, d], log_dt[0, d], b_re[0, d],
                                             b_im[0, d], c_re[0, d], c_im[0, d])
        w['s5_w' + name], w['s5_a' + name], w['s5_c' + name] = wexp, a, cmat
    w['d_skip'] = d_skip[0].reshape(1, SSM_WIDTH)
    w['w_glu'] = w_glu[0].astype(BF16)
    w['b_glu'] = b_glu[0][None, :]
    mla_w = N_HEADS * V_HEAD
    w['w_out_att'] = w_out[0][:mla_w].astype(BF16)
    w['w_out_ssm'] = w_out[0][mla_w:].astype(BF16)
    w['ffn_norm'] = ffn_norm[0][None, :]
    w['w_query'] = w_query[0].astype(BF16)
    w['sub_keys'] = sub_keys[0].astype(BF16)
    w['expert_u'] = expert_u[0].astype(BF16)
    w['expert_vt'] = expert_v[0].astype(BF16).T
    w['final_norm'] = final_norm[None, :]
    return w


def _rope_table(s):
    inv = ROPE_THETA ** (-jnp.arange(0, QK_ROPE, 2, dtype=F32) / QK_ROPE)
    ang = jnp.arange(s, dtype=F32)[:, None] * inv[None, :]
    c, sn = jnp.cos(ang), jnp.sin(ang)
    return jnp.concatenate([c, c, sn, sn], axis=1)


def _encoder(x, w):
    bsz, s, d = x.shape
    q, k, v, u = _in_stage(x, _rope_table(s), w)
    att = _attention(q, k, v)
    yf, yb = _s5_scan(u, w)
    n = bsz * s
    x1, xn, qp = _mid_stage(x.reshape(n, d), att.reshape(n, -1), yf.reshape(n, -1),
                            yb.reshape(n, -1), u.reshape(n, -1), w)
    n1, e1, r2, e2 = _route(qp, w['sub_keys'])
    y = _peer(xn, x1, n1, e1, r2, e2, w)
    return y.reshape(bsz, s, d)


def kernel(x_prompt, x_sample, mix_norm, w_in, q_norm, kv_norm, w_uq, w_ukv, lam_re, lam_im, log_dt, b_re, b_im, c_re, c_im, d_skip, w_glu, b_glu, w_out, ffn_norm, w_query, sub_keys, expert_u, expert_v, final_norm):
    w = _prepare(mix_norm, w_in, q_norm, kv_norm, w_uq, w_ukv, lam_re, lam_im, log_dt, b_re, b_im,
                 c_re, c_im, d_skip, w_glu, b_glu, w_out, ffn_norm, w_query, sub_keys, expert_u,
                 expert_v, final_norm)
    return (_encoder(x_prompt, w), _encoder(x_sample, w))
```

```python
import functools
import math

import jax
import jax.numpy as jnp
from jax import lax
from jax.experimental import pallas as pl
from jax.experimental.pallas import tpu as pltpu

F32 = jnp.float32
BF16 = jnp.bfloat16

EPS = 1e-6
ROPE_THETA = 10000.0
LANES = 128
SUBLANES = 8
VMEM_LIMIT = 56 * 1024 * 1024

N_HEADS = 4
QK_NOPE = 128
QK_ROPE = 64
V_HEAD = 128
QK_PAD = 256
Q_LORA = 384
KV_LORA = 256
SSM_WIDTH = 512
SSM_GROUPS = 32
SSM_GROUP_CH = 16
SSM_STATE = 64
SSM_CHUNKS = SSM_GROUPS * SSM_STATE // LANES
PEER_HEADS = 8
PEER_NKEYS = 128
PEER_TOPK = 16
GELU_C = math.sqrt(2.0 / math.pi)


def _gelu(x):
    return 0.5 * x * (1.0 + jnp.tanh(GELU_C * (x + 0.044715 * (x * x * x))))


def _rms(x, g):
    return x * lax.rsqrt(jnp.mean(x * x, axis=-1, keepdims=True) + EPS) * g


def _params(*sem):
    return pltpu.CompilerParams(dimension_semantics=sem, vmem_limit_bytes=VMEM_LIMIT)


def _in_kernel(x_ref, cs_ref, g_ref, win_ref, qn_ref, kvn_ref, wq_ref, wkv_ref,
               q_ref, k_ref, v_ref, u_ref):
    x = x_ref[...]
    h = _rms(x, g_ref[...])
    proj = jnp.dot(h.astype(BF16), win_ref[...], preferred_element_type=F32)
    o1 = Q_LORA
    o2 = o1 + KV_LORA
    o3 = o2 + 2 * QK_ROPE
    cq = proj[:, :o1]
    ckv = proj[:, o1:o2]
    kp = proj[:, o2:o3]
    u_ref[...] = proj[:, o3:]
    cs = cs_ref[...]
    lane = lax.broadcasted_iota(jnp.int32, kp.shape, 1)
    t = kp * cs
    kpe = jnp.where(lane < QK_ROPE, t + pltpu.roll(t, QK_ROPE, axis=1), 0.0)
    scale = (QK_NOPE + QK_ROPE) ** -0.5
    q = jnp.dot(_rms(cq, qn_ref[...]).astype(BF16), wq_ref[...], preferred_element_type=F32) * scale
    kv = jnp.dot(_rms(ckv, kvn_ref[...]).astype(BF16), wkv_ref[...], preferred_element_type=F32)
    for hd in range(N_HEADS):
        b = hd * QK_PAD
        tq = q[:, b + QK_NOPE:b + QK_PAD] * cs
        q_ref[hd, :, :QK_NOPE] = q[:, b:b + QK_NOPE].astype(BF16)
        q_ref[hd, :, QK_NOPE:] = (tq + pltpu.roll(tq, QK_ROPE, axis=1)).astype(BF16)
        k_ref[hd, :, :QK_NOPE] = kv[:, b:b + QK_NOPE].astype(BF16)
        k_ref[hd, :, QK_NOPE:] = kpe.astype(BF16)
        v_ref[hd] = kv[:, b + QK_NOPE:b + QK_PAD].astype(BF16)


def _in_stage(x, cs, w, tt=1024):
    bsz, s, d = x.shape
    tt = min(tt, s)
    grid = (bsz, s // tt)
    full = lambda a: pl.BlockSpec(a.shape, lambda b, i: (0,) * a.ndim)
    hs = lambda wd: pl.BlockSpec((None, N_HEADS, tt, wd), lambda b, i: (b, 0, i, 0))
    return pl.pallas_call(
        _in_kernel,
        grid=grid,
        in_specs=[pl.BlockSpec((None, tt, d), lambda b, i: (b, i, 0)),
                  pl.BlockSpec((tt, 2 * QK_ROPE), lambda b, i: (i, 0)),
                  full(w['mix_norm']), full(w['w_in']), full(w['q_norm']), full(w['kv_norm']),
                  full(w['w_q']), full(w['w_kv'])],
        out_specs=[hs(QK_PAD), hs(QK_PAD), hs(V_HEAD),
                   pl.BlockSpec((None, tt, SSM_WIDTH), lambda b, i: (b, i, 0))],
        out_shape=[jax.ShapeDtypeStruct((bsz, N_HEADS, s, QK_PAD), BF16),
                   jax.ShapeDtypeStruct((bsz, N_HEADS, s, QK_PAD), BF16),
                   jax.ShapeDtypeStruct((bsz, N_HEADS, s, V_HEAD), BF16),
                   jax.ShapeDtypeStruct((bsz, s, SSM_WIDTH), F32)],
        compiler_params=_params("parallel", "parallel"),
        name="in_stage",
    )(x, cs, w['mix_norm'], w['w_in'], w['q_norm'], w['kv_norm'], w['w_q'], w['w_kv'])


def _flash_kernel(q_ref, k_ref, v_ref, o_ref, m_ref, acc_ref, s0, s1, p0, p1, al0, al1, *, tk):
    n = k_ref.shape[0] // tk
    m_ref[...] = jnp.full(m_ref.shape, -jnp.inf, F32)
    acc_ref[...] = jnp.zeros(acc_ref.shape, F32)
    ones = jnp.ones((tk, LANES), BF16)
    s_buf, p_buf, al_buf = (s0, s1), (p0, p1), (al0, al1)

    def scores(j, slot):
        kj = k_ref[pl.ds(pl.multiple_of(j * tk, tk), tk), :]
        s_buf[slot][...] = lax.dot_general(q_ref[...], kj, (((1,), (1,)), ((), ())),
                                           preferred_element_type=F32)

    def exps(slot):
        s = s_buf[slot][...]
        m_prev = m_ref[...]
        m_new = jnp.maximum(m_prev, jnp.max(s, axis=1, keepdims=True))
        p_buf[slot][...] = jnp.exp(s - jnp.concatenate([m_new] * (tk // LANES), axis=1)).astype(BF16)
        al_buf[slot][...] = jnp.exp(m_prev - m_new)
        m_ref[...] = m_new

    def values(j, slot):
        vj = jnp.concatenate([v_ref[pl.ds(pl.multiple_of(j * tk, tk), tk), :], ones], axis=1)
        acc_ref[...] = (jnp.concatenate([al_buf[slot][...]] * 2, axis=1) * acc_ref[...]
                        + jnp.dot(p_buf[slot][...], vj, preferred_element_type=F32))

    scores(0, 0)
    scores(1, 1)
    exps(0)

    def body(jj, carry):
        j = 2 * jj + 2
        scores(j, 0)
        exps(1)
        values(j - 2, 0)
        scores(j + 1, 1)
        exps(0)
        values(j - 1, 1)
        return carry

    lax.fori_loop(0, (n - 2) // 2, body, 0, unroll=True)
    exps(1)
    values(n - 2, 0)
    values(n - 1, 1)
    o_ref[...] = (acc_ref[:, :V_HEAD] / acc_ref[:, V_HEAD:]).astype(o_ref.dtype)


def _attention(q, k, v, tq=1024, tk=2048):
    bsz, nh, s, _ = q.shape
    tq = min(tq, s)
    tk = min(tk, s // 2)
    assert s % (2 * tk) == 0 and s % tq == 0
    return pl.pallas_call(
        functools.partial(_flash_kernel, tk=tk),
        grid=(bsz, nh, s // tq),
        in_specs=[pl.BlockSpec((None, None, tq, QK_PAD), lambda b, h, i: (b, h, i, 0)),
                  pl.BlockSpec((None, None, s, QK_PAD), lambda b, h, i: (b, h, 0, 0)),
                  pl.BlockSpec((None, None, s, V_HEAD), lambda b, h, i: (b, h, 0, 0))],
        out_specs=pl.BlockSpec((None, tq, V_HEAD), lambda b, h, i: (b, i, h)),
        out_shape=jax.ShapeDtypeStruct((bsz, s, nh * V_HEAD), BF16),
        scratch_shapes=[pltpu.VMEM((tq, LANES), F32), pltpu.VMEM((tq, 2 * V_HEAD), F32),
                        pltpu.VMEM((tq, tk), F32), pltpu.VMEM((tq, tk), F32),
                        pltpu.VMEM((tq, tk), BF16), pltpu.VMEM((tq, tk), BF16),
                        pltpu.VMEM((tq, LANES), F32), pltpu.VMEM((tq, LANES), F32)],
        compiler_params=_params("parallel", "parallel", "arbitrary"),
        name="attention",
    )(q, k, v)


def _s5_kernel(uf_ref, ub_ref, wf_ref, wb_ref, af_ref, ab_ref, cf_ref, cb_ref,
               yf_ref, yb_ref, fre, fim, bre, bim, sfre, sfim, sbre, sbim, st_ref, *, tc, pitch):
    half = SSM_CHUNKS // 2 * pitch

    @pl.when(pl.program_id(1) == 0)
    def _():
        st_ref[...] = jnp.zeros(st_ref.shape, F32)

    for u_ref, w_ref, re, im in ((uf_ref, wf_ref, fre, fim), (ub_ref, wb_ref, bre, bim)):
        for blk in range(SSM_WIDTH // LANES):
            ub = u_ref[:, blk * LANES:(blk + 1) * LANES].astype(BF16)
            for cc in range(SSM_CHUNKS * LANES // SSM_WIDTH):
                ch = blk * (SSM_CHUNKS * LANES // SSM_WIDTH) + cc
                r = jnp.dot(ub, w_ref[ch], preferred_element_type=F32)
                re[pl.ds(ch * pitch, tc), :] = r[:, :LANES]
                im[pl.ds(ch * pitch, tc), :] = r[:, LANES:]

    af = af_ref[...]
    ab = ab_ref[...]

    def step(a, st, re, im, sre, sim, t):
        out = []
        for hf in range(2):
            rows = pl.ds(hf * half + t, SUBLANES, stride=pitch)
            a_re, a_im = a[hf], a[2 + hf]
            s_re, s_im = st[hf], st[2 + hf]
            n_re = a_re * s_re - a_im * s_im + re[rows, :]
            n_im = a_re * s_im + a_im * s_re + im[rows, :]
            sre[rows, :] = n_re
            sim[rows, :] = n_im
            out.append((n_re, n_im))
        return (out[0][0], out[1][0], out[0][1], out[1][1])

    def body(t, carry):
        sf, sb = carry
        sf = step(af, sf, fre, fim, sfre, sfim, t)
        sb = step(ab, sb, bre, bim, sbre, sbim, tc - 1 - t)
        return (sf, sb)

    st = st_ref[...]
    init = (tuple(st[0, i] for i in range(4)), tuple(st[1, i] for i in range(4)))
    sf, sb = lax.fori_loop(0, tc, body, init, unroll=8)
    for i in range(4):
        st_ref[0, i] = sf[i]
        st_ref[1, i] = sb[i]

    for y_ref, c_ref, re, im in ((yf_ref, cf_ref, sfre, sfim), (yb_ref, cb_ref, sbre, sbim)):
        for blk in range(SSM_WIDTH // LANES):
            acc = None
            for cc in range(SSM_CHUNKS * LANES // SSM_WIDTH):
                ch = blk * (SSM_CHUNKS * LANES // SSM_WIDTH) + cc
                s = jnp.concatenate([re[pl.ds(ch * pitch, tc), :].astype(BF16),
                                     im[pl.ds(ch * pitch, tc), :].astype(BF16)], axis=1)
                d = jnp.dot(s, c_ref[ch], preferred_element_type=F32)
                acc = d if acc is None else acc + d
            y_ref[:, blk * LANES:(blk + 1) * LANES] = acc


def _s5_scan(u, w, tc=512):
    bsz, s, _ = u.shape
    tc = min(tc, s)
    n = s // tc
    pitch = tc + SUBLANES // 2
    rows = SSM_CHUNKS * pitch
    full = lambda a: pl.BlockSpec(a.shape, lambda b, c: (0,) * a.ndim)
    ublk = lambda f: pl.BlockSpec((None, tc, SSM_WIDTH), f)
    fwd = lambda b, c: (b, c, 0)
    bwd = lambda b, c: (b, n - 1 - c, 0)
    return pl.pallas_call(
        functools.partial(_s5_kernel, tc=tc, pitch=pitch),
        grid=(bsz, n),
        in_specs=[ublk(fwd), ublk(bwd), full(w['s5_wf']), full(w['s5_wb']),
                  full(w['s5_af']), full(w['s5_ab']), full(w['s5_cf']), full(w['s5_cb'])],
        out_specs=[ublk(fwd), ublk(bwd)],
        out_shape=[jax.ShapeDtypeStruct(u.shape, F32), jax.ShapeDtypeStruct(u.shape, F32)],
        scratch_shapes=[pltpu.VMEM((rows, LANES), F32) for _ in range(8)]
                       + [pltpu.VMEM((2, 4, SUBLANES, LANES), F32)],
        compiler_params=_params("parallel", "arbitrary"),
        name="s5_scan",
    )(u, u, w['s5_wf'], w['s5_wb'], w['s5_af'], w['s5_ab'], w['s5_cf'], w['s5_cb'])


def _mid_kernel(x_ref, att_ref, yf_ref, yb_ref, u_ref, dsk_ref, wglu_ref, bglu_ref,
                woa_ref, wos_ref, fn_ref, wqry_ref, x1_ref, xn_ref, qp_ref):
    y = dsk_ref[...] * u_ref[...] + yf_ref[...] + yb_ref[...]
    z = _gelu(y)
    gate = jnp.dot(z.astype(BF16), wglu_ref[...], preferred_element_type=F32) + bglu_ref[...]
    z = z * (1.0 / (1.0 + jnp.exp(-gate)))
    mix = (jnp.dot(att_ref[...], woa_ref[...], preferred_element_type=F32)
           + jnp.dot(z.astype(BF16), wos_ref[...], preferred_element_type=F32))
    x1 = x_ref[...] + mix
    x1_ref[...] = x1
    xn = _rms(x1, fn_ref[...]).astype(BF16)
    xn_ref[...] = xn
    qp_ref[...] = jnp.dot(xn, wqry_ref[...], preferred_element_type=F32).astype(BF16)


def _mid_stage(x, att, yf, yb, u, w, tt=1024):
    n, d = x.shape
    tt = min(tt, n)
    full = lambda a: pl.BlockSpec(a.shape, lambda i: (0,) * a.ndim)
    row = lambda wd: pl.BlockSpec((tt, wd), lambda i: (i, 0))
    nq = w['w_query'].shape[1]
    return pl.pallas_call(
        _mid_kernel,
        grid=(n // tt,),
        in_specs=[row(d), row(att.shape[1]), row(SSM_WIDTH), row(SSM_WIDTH), row(SSM_WIDTH),
                  full(w['d_skip']), full(w['w_glu']), full(w['b_glu']), full(w['w_out_att']),
                  full(w['w_out_ssm']), full(w['ffn_norm']), full(w['w_query'])],
        out_specs=[row(d), row(d), row(nq)],
        out_shape=[jax.ShapeDtypeStruct((n, d), F32), jax.ShapeDtypeStruct((n, d), BF16),
                   jax.ShapeDtypeStruct((n, nq), BF16)],
        compiler_params=_params("parallel"),
        name="mid_stage",
    )(x, att, yf, yb, u, w['d_skip'], w['w_glu'], w['b_glu'], w['w_out_att'], w['w_out_ssm'],
      w['ffn_norm'], w['w_query'])


def _merge_desc(x):
    n = len(x)
    if n == 1:
        return x
    h = n // 2
    hi = [jnp.maximum(x[i], x[i + h]) for i in range(h)]
    lo = [jnp.minimum(x[i], x[i + h]) for i in range(h)]
    return _merge_desc(hi) + _merge_desc(lo)


def _sort_desc(x):
    n = len(x)
    if n == 1:
        return x
    h = n // 2
    return _merge_desc(_sort_desc(x[:h]) + _sort_desc(x[h:])[::-1])


def _merge_top(a, b):
    k = len(a)
    c = [jnp.maximum(a[i], b[k - 1 - i]) if k - 1 - i < len(b) else a[i] for i in range(k)]
    return _merge_desc(c)


def _route_kernel(qp_ref, keys_ref, n1_ref, e1_ref, r2_ref, e2_ref, sc_ref, out_ref, *, pitch):
    k_top = PEER_TOPK
    nk = PEER_NKEYS
    nsub = qp_ref.shape[0] // LANES

    for side in range(2):
        kmat = keys_ref[side]
        for j in range(nsub):
            qj = qp_ref[j * LANES:(j + 1) * LANES, side * LANES:(side + 1) * LANES]
            sc_ref[side, pl.ds(j * pitch, nk), :] = lax.dot_general(
                kmat, qj, (((1,), (1,)), ((), ())), preferred_element_type=F32)

    def key_rows(kk):
        return pl.ds(kk, nsub, stride=pitch)

    def top_sorted(side):
        best = None
        for g in range(nk // k_top):
            grp = _sort_desc([sc_ref[side, key_rows(g * k_top + i), :] for i in range(k_top)])
            best = grp if best is None else _merge_top(best, grp)
        return best

    v1 = top_sorted(0)
    v2 = top_sorted(1)

    lens = [k_top // (a + 1) for a in range(k_top)]
    cand = [[v1[a] + v2[b] for b in range(lens[a])] for a in range(k_top)]
    top = cand[0]
    a = 1
    while lens[a] > 1:
        top = _merge_top(top, cand[a])
        a += 1
    top = _merge_top(top, [cand[i][0] for i in range(a, k_top)])
    thr = top[k_top - 1]

    m = cand[0][0]
    z = None
    for row in cand:
        for c in row:
            e = jnp.where(c >= thr, jnp.exp(c - m), 0.0)
            z = e if z is None else z + e
    inv_z = 0.5 / z

    n_big = 3
    big = []
    for a in range(n_big):
        na = jnp.zeros_like(thr)
        for c in cand[a]:
            na = na + jnp.where(c >= thr, 1.0, 0.0)
        big.append(na)

    inf = jnp.full(thr.shape, jnp.inf, F32)
    tb = []
    for b in range(lens[n_big]):
        t = inf
        for a in range(k_top):
            if b < lens[a]:
                t = jnp.minimum(t, jnp.where(cand[a][b] >= thr, v1[a], inf))
        tb.append(t)

    def bf16_bits(x):
        return pltpu.bitcast(x.astype(BF16).astype(F32), jnp.uint32)

    def pack2(lo, hi):
        return pltpu.bitcast((bf16_bits(lo) >> 16) | bf16_bits(hi), F32)

    for kk in range(nk):
        rows = key_rows(kk)
        s1 = sc_ref[0, rows, :]
        cnt = jnp.zeros_like(s1)
        for b in range(lens[n_big]):
            cnt = jnp.where(s1 >= tb[b], float(b + 1), cnt)
        for a in range(n_big - 1, -1, -1):
            cnt = jnp.where(s1 == v1[a], big[a], cnt)
        out_ref[0, rows, :] = pack2(cnt, cnt)
        g1 = jnp.exp(s1 - v1[0]) * inv_z
        out_ref[1, rows, :] = pack2(g1, g1)

    def second(kk):
        s2 = sc_ref[1, key_rows(kk), :]
        rank = jnp.full(s2.shape, float(k_top), F32)
        for b in range(k_top - 1, -1, -1):
            rank = jnp.where(s2 >= v2[b], float(b), rank)
        return rank, jnp.exp(s2 - v2[0])

    for kp in range(nk // 2):
        r_lo, g_lo = second(2 * kp)
        r_hi, g_hi = second(2 * kp + 1)
        out_ref[2, key_rows(kp), :] = pack2(r_lo, r_hi)
        out_ref[3, key_rows(kp), :] = pack2(g_lo, g_hi)

    for j in range(nsub):
        cols = slice(j * LANES, (j + 1) * LANES)
        n1_ref[:, cols] = out_ref[0, pl.ds(j * pitch, nk), :]
        e1_ref[:, cols] = out_ref[1, pl.ds(j * pitch, nk), :]
        r2_ref[:, cols] = out_ref[2, pl.ds(j * pitch, nk // 2), :]
        e2_ref[:, cols] = out_ref[3, pl.ds(j * pitch, nk // 2), :]


def _route(qp, keys):
    n = qp.shape[0]
    tt = min(SUBLANES * LANES, n)
    nsub = tt // LANES
    pitch = PEER_NKEYS + SUBLANES
    rows = nsub * pitch
    hk = 2 * LANES
    nk, nk2 = PEER_NKEYS, PEER_NKEYS // 2
    blk = lambda r: pl.BlockSpec((None, r, tt), lambda i, h: (h, 0, i))
    shp = lambda r: jax.ShapeDtypeStruct((PEER_HEADS, r, n), F32)
    return pl.pallas_call(
        functools.partial(_route_kernel, pitch=pitch),
        grid=(n // tt, PEER_HEADS),
        in_specs=[pl.BlockSpec((tt, hk), lambda i, h: (i, h)),
                  pl.BlockSpec((2, None, PEER_NKEYS, LANES), lambda i, h: (0, h, 0, 0))],
        out_specs=[blk(nk), blk(nk), blk(nk2), blk(nk2)],
        out_shape=[shp(nk), shp(nk), shp(nk2), shp(nk2)],
        scratch_shapes=[pltpu.VMEM((2, rows, LANES), F32), pltpu.VMEM((4, rows, LANES), F32)],
        compiler_params=_params("parallel", "parallel"),
        name="peer_route",
    )(qp, keys)


def _peer_kernel(xn_ref, x1_ref, n1_ref, e1_ref, r2_ref, e2_ref, u_ref, vt_ref, fn_ref,
                 y_ref, a_ref, hid_ref, acc_ref, *, tcol):
    e = pl.program_id(1)
    tt = xn_ref.shape[0]
    rows_per_tile = u_ref.shape[0] // PEER_NKEYS

    @pl.when(e == 0)
    def _():
        acc_ref[...] = jnp.zeros(acc_ref.shape, F32)

    a_ref[...] = lax.dot_general(u_ref[...], xn_ref[...], (((1,), (1,)), ((), ())),
                                 preferred_element_type=F32)

    def bcast_row(ref, h, row, cols):
        w = jnp.broadcast_to(ref[h, row:row + 1, cols], (SUBLANES, tcol))
        return jnp.concatenate([pltpu.bitcast(w, BF16)] * (PEER_NKEYS // (2 * SUBLANES)), axis=0)

    def keys_tile(ref, h, cols):
        return pltpu.bitcast(ref[h, :, cols], BF16)

    for i1 in range(rows_per_tile):
        for c in range(tt // tcol):
            cols = slice(c * tcol, (c + 1) * tcol)
            g = None
            for h in range(PEER_HEADS):
                sel = (jnp.where(keys_tile(r2_ref, h, cols) < bcast_row(n1_ref, h, i1, cols),
                                 keys_tile(e2_ref, h, cols), jnp.zeros((), BF16))
                       * bcast_row(e1_ref, h, i1, cols))
                g = sel if g is None else g + sel
            a = a_ref[i1 * PEER_NKEYS:(i1 + 1) * PEER_NKEYS, cols].astype(BF16)
            th = jnp.tanh(a * (GELU_C + (GELU_C * 0.044715) * (a * a)))
            hid_ref[i1 * PEER_NKEYS:(i1 + 1) * PEER_NKEYS, cols] = (a * (1.0 + th)) * g

    acc_ref[...] += jnp.dot(vt_ref[...], hid_ref[...], preferred_element_type=F32)

    @pl.when(e == pl.num_programs(1) - 1)
    def _():
        x2 = x1_ref[...] + acc_ref[...].T
        y_ref[...] = _rms(x2, fn_ref[...])


def _peer(xn, x1, n1, e1, r2, e2, w, tt=1024, te=1024, tcol=1024):
    n, d = xn.shape
    tt = min(tt, n)
    tcol = min(tcol, tt)
    ne = w['expert_u'].shape[0]
    rt = lambda: pl.BlockSpec((PEER_HEADS, PEER_NKEYS // 2, tt), lambda i, e: (0, 0, i))
    r1 = lambda: pl.BlockSpec((PEER_HEADS, te // PEER_NKEYS, tt), lambda i, e: (0, e, i))
    return pl.pallas_call(
        functools.partial(_peer_kernel, tcol=tcol),
        grid=(n // tt, ne // te),
        in_specs=[pl.BlockSpec((tt, d), lambda i, e: (i, 0)),
                  pl.BlockSpec((tt, d), lambda i, e: (i, 0)),
                  r1(), r1(), rt(), rt(),
                  pl.BlockSpec((te, d), lambda i, e: (e, 0)),
                  pl.BlockSpec((d, te), lambda i, e: (0, e)),
                  pl.BlockSpec((1, d), lambda i, e: (0, 0))],
        out_specs=pl.BlockSpec((tt, d), lambda i, e: (i, 0)),
        out_shape=jax.ShapeDtypeStruct((n, d), F32),
        scratch_shapes=[pltpu.VMEM((te, tt), F32), pltpu.VMEM((te, tt), BF16),
                        pltpu.VMEM((d, tt), F32)],
        compiler_params=_params("parallel", "arbitrary"),
        name="peer_experts",
    )(xn, x1, n1, e1, r2, e2, w['expert_u'], w['expert_vt'], w['final_norm'])


def _rot_cols(wm):
    hw = wm.shape[-1] // 2
    return jnp.concatenate([-wm[..., hw:], wm[..., :hw]], axis=-1)


def _s5_direction_params(lam_re, lam_im, log_dt, b_re, b_im, c_re, c_im):
    g, p, ch = SSM_GROUPS, SSM_STATE, SSM_GROUP_CH
    dt = jnp.exp(log_dt)[:, None]
    mag = jnp.exp(lam_re * dt)
    a_re = mag * jnp.cos(lam_im * dt)
    a_im = mag * jnp.sin(lam_im * dt)
    nr = a_re - 1.0
    den = lam_re * lam_re + lam_im * lam_im
    f_re = (nr * lam_re + a_im * lam_im) / den
    f_im = (a_im * lam_re - nr * lam_im) / den
    bb_re = f_re[..., None] * b_re - f_im[..., None] * b_im
    bb_im = f_re[..., None] * b_im + f_im[..., None] * b_re
    eye = jnp.eye(g, dtype=F32)
    per_blk = SSM_CHUNKS * LANES // SSM_WIDTH

    def expand(bb):
        nb = SSM_WIDTH // LANES
        m = jnp.einsum('gph,gk->ghkp', bb, eye).reshape(nb, LANES, nb, per_blk, LANES)
        return jnp.einsum('arack->acrk', m).reshape(SSM_CHUNKS, LANES, LANES)

    def readout(cm):
        nb = SSM_WIDTH // LANES
        m = jnp.einsum('ghp,gk->gpkh', cm, eye).reshape(nb, per_blk, LANES, nb, LANES)
        return jnp.einsum('acrak->acrk', m).reshape(SSM_CHUNKS, LANES, LANES)

    wexp = jnp.concatenate([expand(bb_re), expand(bb_im)], axis=-1).astype(BF16)
    cmat = jnp.concatenate([readout(c_re), readout(-c_im)], axis=1).astype(BF16)
    a = jnp.concatenate([a_re.reshape(2, SUBLANES, LANES), a_im.reshape(2, SUBLANES, LANES)])
    return wexp, a, cmat


def _prepare(mix_norm, w_in, q_norm, kv_norm, w_uq, w_ukv, lam_re, lam_im, log_dt, b_re, b_im,
             c_re, c_im, d_skip, w_glu, b_glu, w_out, ffn_norm, w_query, sub_keys, expert_u,
             expert_v, final_norm):
    o1 = Q_LORA
    o2 = o1 + KV_LORA
    o3 = o2 + QK_ROPE
    wi = w_in[0]
    w_kpe = wi[:, o2:o3]
    w = {}
    w['mix_norm'] = mix_norm[0][None, :]
    w['w_in'] = jnp.concatenate([wi[:, :o2], w_kpe, _rot_cols(w_kpe), wi[:, o3:]], axis=1).astype(BF16)
    w['q_norm'] = q_norm[0][None, :]
    w['kv_norm'] = kv_norm[0][None, :]
    wq = w_uq[0]
    wq_pe = wq[..., QK_NOPE:]
    w['w_q'] = jnp.concatenate([wq[..., :QK_NOPE], wq_pe, _rot_cols(wq_pe)], axis=-1).reshape(
        Q_LORA, N_HEADS * QK_PAD).astype(BF16)
    w['w_kv'] = w_ukv[0].reshape(KV_LORA, N_HEADS * (QK_NOPE + V_HEAD)).astype(BF16)
    for name, d in (('f', 0), ('b', 1)):
        wexp, a, cmat = _s5_direction_params(lam_re[0, d], lam_im[0, d], log_dt[0, d], b_re[0, d],
                                             b_im[0, d], c_re[0, d], c_im[0, d])
        w['s5_w' + name], w['s5_a' + name], w['s5_c' + name] = wexp, a, cmat
    w['d_skip'] = d_skip[0].reshape(1, SSM_WIDTH)
    w['w_glu'] = w_glu[0].astype(BF16)
    w['b_glu'] = b_glu[0][None, :]
    mla_w = N_HEADS * V_HEAD
    w['w_out_att'] = w_out[0][:mla_w].astype(BF16)
    w['w_out_ssm'] = w_out[0][mla_w:].astype(BF16)
    w['ffn_norm'] = ffn_norm[0][None, :]
    w['w_query'] = w_query[0].astype(BF16)
    w['sub_keys'] = sub_keys[0].astype(BF16)
    w['expert_u'] = expert_u[0].astype(BF16)
    w['expert_vt'] = expert_v[0].astype(BF16).T
    w['final_norm'] = final_norm[None, :]
    return w


def _rope_table(s):
    inv = ROPE_THETA ** (-jnp.arange(0, QK_ROPE, 2, dtype=F32) / QK_ROPE)
    ang = jnp.arange(s, dtype=F32)[:, None] * inv[None, :]
    c, sn = jnp.cos(ang), jnp.sin(ang)
    return jnp.concatenate([c, c, sn, sn], axis=1)


def _encoder(x, w):
    bsz, s, d = x.shape
    q, k, v, u = _in_stage(x, _rope_table(s), w)
    att = _attention(q, k, v)
    yf, yb = _s5_scan(u, w)
    n = bsz * s
    x1, xn, qp = _mid_stage(x.reshape(n, d), att.reshape(n, -1), yf.reshape(n, -1),
                            yb.reshape(n, -1), u.reshape(n, -1), w)
    n1, e1, r2, e2 = _route(qp, w['sub_keys'])
    y = _peer(xn, x1, n1, e1, r2, e2, w)
    return y.reshape(bsz, s, d)


def kernel(x_prompt, x_sample, mix_norm, w_in, q_norm, kv_norm, w_uq, w_ukv, lam_re, lam_im, log_dt, b_re, b_im, c_re, c_im, d_skip, w_glu, b_glu, w_out, ffn_norm, w_query, sub_keys, expert_u, expert_v, final_norm):
    w = _prepare(mix_norm, w_in, q_norm, kv_norm, w_uq, w_ukv, lam_re, lam_im, log_dt, b_re, b_im,
                 c_re, c_im, d_skip, w_glu, b_glu, w_out, ffn_norm, w_query, sub_keys, expert_u,
                 expert_v, final_norm)
    return (_encoder(x_prompt, w), _encoder(x_sample, w))
```

```python
import functools
import math

import jax
import jax.numpy as jnp
from jax import lax
from jax.experimental import pallas as pl
from jax.experimental.pallas import tpu as pltpu

F32 = jnp.float32
BF16 = jnp.bfloat16

EPS = 1e-6
ROPE_THETA = 10000.0
LANES = 128
SUBLANES = 8
VMEM_LIMIT = 56 * 1024 * 1024

N_HEADS = 4
QK_NOPE = 128
QK_ROPE = 64
V_HEAD = 128
QK_PAD = 256
Q_LORA = 384
KV_LORA = 256
SSM_WIDTH = 512
SSM_GROUPS = 32
SSM_GROUP_CH = 16
SSM_STATE = 64
SSM_CHUNKS = SSM_GROUPS * SSM_STATE // LANES
PEER_HEADS = 8
PEER_NKEYS = 128
PEER_TOPK = 16
GELU_C = math.sqrt(2.0 / math.pi)


def _gelu(x):
    return 0.5 * x * (1.0 + jnp.tanh(GELU_C * (x + 0.044715 * (x * x * x))))


def _rms(x, g):
    return x * lax.rsqrt(jnp.mean(x * x, axis=-1, keepdims=True) + EPS) * g


def _params(*sem):
    return pltpu.CompilerParams(dimension_semantics=sem, vmem_limit_bytes=VMEM_LIMIT)


def _in_kernel(x_ref, cs_ref, g_ref, win_ref, qn_ref, kvn_ref, wq_ref, wkv_ref,
               q_ref, k_ref, v_ref, u_ref):
    x = x_ref[...]
    h = _rms(x, g_ref[...])
    proj = jnp.dot(h.astype(BF16), win_ref[...], preferred_element_type=F32)
    o1 = Q_LORA
    o2 = o1 + KV_LORA
    o3 = o2 + 2 * QK_ROPE
    cq = proj[:, :o1]
    ckv = proj[:, o1:o2]
    kp = proj[:, o2:o3]
    u_ref[...] = proj[:, o3:]
    cs = cs_ref[...]
    lane = lax.broadcasted_iota(jnp.int32, kp.shape, 1)
    t = kp * cs
    kpe = jnp.where(lane < QK_ROPE, t + pltpu.roll(t, QK_ROPE, axis=1), 0.0)
    scale = (QK_NOPE + QK_ROPE) ** -0.5
    q = jnp.dot(_rms(cq, qn_ref[...]).astype(BF16), wq_ref[...], preferred_element_type=F32) * scale
    kv = jnp.dot(_rms(ckv, kvn_ref[...]).astype(BF16), wkv_ref[...], preferred_element_type=F32)
    for hd in range(N_HEADS):
        b = hd * QK_PAD
        tq = q[:, b + QK_NOPE:b + QK_PAD] * cs
        q_ref[hd, :, :QK_NOPE] = q[:, b:b + QK_NOPE].astype(BF16)
        q_ref[hd, :, QK_NOPE:] = (tq + pltpu.roll(tq, QK_ROPE, axis=1)).astype(BF16)
        k_ref[hd, :, :QK_NOPE] = kv[:, b:b + QK_NOPE].astype(BF16)
        k_ref[hd, :, QK_NOPE:] = kpe.astype(BF16)
        v_ref[hd] = kv[:, b + QK_NOPE:b + QK_PAD].astype(BF16)


def _in_stage(x, cs, w, tt=512):
    bsz, s, d = x.shape
    tt = min(tt, s)
    grid = (bsz, s // tt)
    full = lambda a: pl.BlockSpec(a.shape, lambda b, i: (0,) * a.ndim)
    hs = lambda wd: pl.BlockSpec((None, N_HEADS, tt, wd), lambda b, i: (b, 0, i, 0))
    return pl.pallas_call(
        _in_kernel,
        grid=grid,
        in_specs=[pl.BlockSpec((None, tt, d), lambda b, i: (b, i, 0)),
                  pl.BlockSpec((tt, 2 * QK_ROPE), lambda b, i: (i, 0)),
                  full(w['mix_norm']), full(w['w_in']), full(w['q_norm']), full(w['kv_norm']),
                  full(w['w_q']), full(w['w_kv'])],
        out_specs=[hs(QK_PAD), hs(QK_PAD), hs(V_HEAD),
                   pl.BlockSpec((None, tt, SSM_WIDTH), lambda b, i: (b, i, 0))],
        out_shape=[jax.ShapeDtypeStruct((bsz, N_HEADS, s, QK_PAD), BF16),
                   jax.ShapeDtypeStruct((bsz, N_HEADS, s, QK_PAD), BF16),
                   jax.ShapeDtypeStruct((bsz, N_HEADS, s, V_HEAD), BF16),
                   jax.ShapeDtypeStruct((bsz, s, SSM_WIDTH), F32)],
        compiler_params=_params("parallel", "parallel"),
        name="in_stage",
    )(x, cs, w['mix_norm'], w['w_in'], w['q_norm'], w['kv_norm'], w['w_q'], w['w_kv'])


def _flash_kernel(q_ref, k_ref, v_ref, o_ref, m_ref, acc_ref, s0, s1, p0, p1, al0, al1, *, tk):
    n = k_ref.shape[0] // tk
    m_ref[...] = jnp.full(m_ref.shape, -jnp.inf, F32)
    acc_ref[...] = jnp.zeros(acc_ref.shape, F32)
    ones = jnp.ones((tk, LANES), BF16)
    s_buf, p_buf, al_buf = (s0, s1), (p0, p1), (al0, al1)

    def scores(j, slot):
        kj = k_ref[pl.ds(pl.multiple_of(j * tk, tk), tk), :]
        s_buf[slot][...] = lax.dot_general(q_ref[...], kj, (((1,), (1,)), ((), ())),
                                           preferred_element_type=F32)

    def exps(slot):
        s = s_buf[slot][...]
        m_prev = m_ref[...]
        m_new = jnp.maximum(m_prev, jnp.max(s, axis=1, keepdims=True))
        p_buf[slot][...] = jnp.exp(s - jnp.concatenate([m_new] * (tk // LANES), axis=1)).astype(BF16)
        al_buf[slot][...] = jnp.exp(m_prev - m_new)
        m_ref[...] = m_new

    def values(j, slot):
        vj = jnp.concatenate([v_ref[pl.ds(pl.multiple_of(j * tk, tk), tk), :], ones], axis=1)
        acc_ref[...] = (jnp.concatenate([al_buf[slot][...]] * 2, axis=1) * acc_ref[...]
                        + jnp.dot(p_buf[slot][...], vj, preferred_element_type=F32))

    scores(0, 0)
    scores(1, 1)
    exps(0)

    def body(jj, carry):
        j = 2 * jj + 2
        scores(j, 0)
        exps(1)
        values(j - 2, 0)
        scores(j + 1, 1)
        exps(0)
        values(j - 1, 1)
        return carry

    lax.fori_loop(0, (n - 2) // 2, body, 0, unroll=True)
    exps(1)
    values(n - 2, 0)
    values(n - 1, 1)
    o_ref[...] = (acc_ref[:, :V_HEAD] / acc_ref[:, V_HEAD:]).astype(o_ref.dtype)


def _attention(q, k, v, tq=1024, tk=2048):
    bsz, nh, s, _ = q.shape
    tq = min(tq, s)
    tk = min(tk, s // 2)
    assert s % (2 * tk) == 0 and s % tq == 0
    return pl.pallas_call(
        functools.partial(_flash_kernel, tk=tk),
        grid=(bsz, nh, s // tq),
        in_specs=[pl.BlockSpec((None, None, tq, QK_PAD), lambda b, h, i: (b, h, i, 0)),
                  pl.BlockSpec((None, None, s, QK_PAD), lambda b, h, i: (b, h, 0, 0)),
                  pl.BlockSpec((None, None, s, V_HEAD), lambda b, h, i: (b, h, 0, 0))],
        out_specs=pl.BlockSpec((None, tq, V_HEAD), lambda b, h, i: (b, i, h)),
        out_shape=jax.ShapeDtypeStruct((bsz, s, nh * V_HEAD), BF16),
        scratch_shapes=[pltpu.VMEM((tq, LANES), F32), pltpu.VMEM((tq, 2 * V_HEAD), F32),
                        pltpu.VMEM((tq, tk), F32), pltpu.VMEM((tq, tk), F32),
                        pltpu.VMEM((tq, tk), BF16), pltpu.VMEM((tq, tk), BF16),
                        pltpu.VMEM((tq, LANES), F32), pltpu.VMEM((tq, LANES), F32)],
        compiler_params=_params("parallel", "parallel", "arbitrary"),
        name="attention",
    )(q, k, v)


def _s5_kernel(uf_ref, ub_ref, wf_ref, wb_ref, af_ref, ab_ref, cf_ref, cb_ref,
               yf_ref, yb_ref, fre, fim, bre, bim, sfre, sfim, sbre, sbim, st_ref, *, tc, pitch):
    half = SSM_CHUNKS // 2 * pitch

    @pl.when(pl.program_id(1) == 0)
    def _():
        st_ref[...] = jnp.zeros(st_ref.shape, F32)

    for u_ref, w_ref, re, im in ((uf_ref, wf_ref, fre, fim), (ub_ref, wb_ref, bre, bim)):
        for blk in range(SSM_WIDTH // LANES):
            ub = u_ref[:, blk * LANES:(blk + 1) * LANES].astype(BF16)
            for cc in range(SSM_CHUNKS * LANES // SSM_WIDTH):
                ch = blk * (SSM_CHUNKS * LANES // SSM_WIDTH) + cc
                r = jnp.dot(ub, w_ref[ch], preferred_element_type=F32)
                re[pl.ds(ch * pitch, tc), :] = r[:, :LANES]
                im[pl.ds(ch * pitch, tc), :] = r[:, LANES:]

    af = af_ref[...]
    ab = ab_ref[...]

    def step(a, st, re, im, sre, sim, t):
        out = []
        for hf in range(2):
            rows = pl.ds(hf * half + t, SUBLANES, stride=pitch)
            a_re, a_im = a[hf], a[2 + hf]
            s_re, s_im = st[hf], st[2 + hf]
            n_re = a_re * s_re - a_im * s_im + re[rows, :]
            n_im = a_re * s_im + a_im * s_re + im[rows, :]
            sre[rows, :] = n_re
            sim[rows, :] = n_im
            out.append((n_re, n_im))
        return (out[0][0], out[1][0], out[0][1], out[1][1])

    def body(t, carry):
        sf, sb = carry
        sf = step(af, sf, fre, fim, sfre, sfim, t)
        sb = step(ab, sb, bre, bim, sbre, sbim, tc - 1 - t)
        return (sf, sb)

    st = st_ref[...]
    init = (tuple(st[0, i] for i in range(4)), tuple(st[1, i] for i in range(4)))
    sf, sb = lax.fori_loop(0, tc, body, init, unroll=8)
    for i in range(4):
        st_ref[0, i] = sf[i]
        st_ref[1, i] = sb[i]

    for y_ref, c_ref, re, im in ((yf_ref, cf_ref, sfre, sfim), (yb_ref, cb_ref, sbre, sbim)):
        for blk in range(SSM_WIDTH // LANES):
            acc = None
            for cc in range(SSM_CHUNKS * LANES // SSM_WIDTH):
                ch = blk * (SSM_CHUNKS * LANES // SSM_WIDTH) + cc
                s = jnp.concatenate([re[pl.ds(ch * pitch, tc), :].astype(BF16),
                                     im[pl.ds(ch * pitch, tc), :].astype(BF16)], axis=1)
                d = jnp.dot(s, c_ref[ch], preferred_element_type=F32)
                acc = d if acc is None else acc + d
            y_ref[:, blk * LANES:(blk + 1) * LANES] = acc


def _s5_scan(u, w, tc=512):
    bsz, s, _ = u.shape
    tc = min(tc, s)
    n = s // tc
    pitch = tc + SUBLANES // 2
    rows = SSM_CHUNKS * pitch
    full = lambda a: pl.BlockSpec(a.shape, lambda b, c: (0,) * a.ndim)
    ublk = lambda f: pl.BlockSpec((None, tc, SSM_WIDTH), f)
    fwd = lambda b, c: (b, c, 0)
    bwd = lambda b, c: (b, n - 1 - c, 0)
    return pl.pallas_call(
        functools.partial(_s5_kernel, tc=tc, pitch=pitch),
        grid=(bsz, n),
        in_specs=[ublk(fwd), ublk(bwd), full(w['s5_wf']), full(w['s5_wb']),
                  full(w['s5_af']), full(w['s5_ab']), full(w['s5_cf']), full(w['s5_cb'])],
        out_specs=[ublk(fwd), ublk(bwd)],
        out_shape=[jax.ShapeDtypeStruct(u.shape, F32), jax.ShapeDtypeStruct(u.shape, F32)],
        scratch_shapes=[pltpu.VMEM((rows, LANES), F32) for _ in range(8)]
                       + [pltpu.VMEM((2, 4, SUBLANES, LANES), F32)],
        compiler_params=_params("parallel", "arbitrary"),
        name="s5_scan",
    )(u, u, w['s5_wf'], w['s5_wb'], w['s5_af'], w['s5_ab'], w['s5_cf'], w['s5_cb'])


def _mid_kernel(x_ref, att_ref, yf_ref, yb_ref, u_ref, dsk_ref, wglu_ref, bglu_ref,
                woa_ref, wos_ref, fn_ref, wqry_ref, x1_ref, xn_ref, qp_ref):
    y = dsk_ref[...] * u_ref[...] + yf_ref[...] + yb_ref[...]
    z = _gelu(y)
    gate = jnp.dot(z.astype(BF16), wglu_ref[...], preferred_element_type=F32) + bglu_ref[...]
    z = z * (1.0 / (1.0 + jnp.exp(-gate)))
    mix = (jnp.dot(att_ref[...], woa_ref[...], preferred_element_type=F32)
           + jnp.dot(z.astype(BF16), wos_ref[...], preferred_element_type=F32))
    x1 = x_ref[...] + mix
    x1_ref[...] = x1
    xn = _rms(x1, fn_ref[...]).astype(BF16)
    xn_ref[...] = xn
    qp_ref[...] = jnp.dot(xn, wqry_ref[...], preferred_element_type=F32).astype(BF16)


def _mid_stage(x, att, yf, yb, u, w, tt=512):
    n, d = x.shape
    tt = min(tt, n)
    full = lambda a: pl.BlockSpec(a.shape, lambda i: (0,) * a.ndim)
    row = lambda wd: pl.BlockSpec((tt, wd), lambda i: (i, 0))
    nq = w['w_query'].shape[1]
    return pl.pallas_call(
        _mid_kernel,
        grid=(n // tt,),
        in_specs=[row(d), row(att.shape[1]), row(SSM_WIDTH), row(SSM_WIDTH), row(SSM_WIDTH),
                  full(w['d_skip']), full(w['w_glu']), full(w['b_glu']), full(w['w_out_att']),
                  full(w['w_out_ssm']), full(w['ffn_norm']), full(w['w_query'])],
        out_specs=[row(d), row(d), row(nq)],
        out_shape=[jax.ShapeDtypeStruct((n, d), F32), jax.ShapeDtypeStruct((n, d), BF16),
                   jax.ShapeDtypeStruct((n, nq), BF16)],
        compiler_params=_params("parallel"),
        name="mid_stage",
    )(x, att, yf, yb, u, w['d_skip'], w['w_glu'], w['b_glu'], w['w_out_att'], w['w_out_ssm'],
      w['ffn_norm'], w['w_query'])


def _merge_desc(x):
    n = len(x)
    if n == 1:
        return x
    h = n // 2
    hi = [jnp.maximum(x[i], x[i + h]) for i in range(h)]
    lo = [jnp.minimum(x[i], x[i + h]) for i in range(h)]
    return _merge_desc(hi) + _merge_desc(lo)


def _sort_desc(x):
    n = len(x)
    if n == 1:
        return x
    h = n // 2
    return _merge_desc(_sort_desc(x[:h]) + _sort_desc(x[h:])[::-1])


def _merge_top(a, b):
    k = len(a)
    c = [jnp.maximum(a[i], b[k - 1 - i]) if k - 1 - i < len(b) else a[i] for i in range(k)]
    return _merge_desc(c)


def _route_kernel(qp_ref, keys_ref, n1_ref, e1_ref, r2_ref, e2_ref, sc_ref, out_ref, *, pitch):
    k_top = PEER_TOPK
    nk = PEER_NKEYS
    nsub = qp_ref.shape[0] // LANES

    for side in range(2):
        kmat = keys_ref[side]
        for j in range(nsub):
            qj = qp_ref[j * LANES:(j + 1) * LANES, side * LANES:(side + 1) * LANES]
            sc_ref[side, pl.ds(j * pitch, nk), :] = lax.dot_general(
                kmat, qj, (((1,), (1,)), ((), ())), preferred_element_type=F32)

    def key_rows(kk):
        return pl.ds(kk, nsub, stride=pitch)

    def top_sorted(side):
        best = None
        for g in range(nk // k_top):
            grp = _sort_desc([sc_ref[side, key_rows(g * k_top + i), :] for i in range(k_top)])
            best = grp if best is None else _merge_top(best, grp)
        return best

    v1 = top_sorted(0)
    v2 = top_sorted(1)

    lens = [k_top // (a + 1) for a in range(k_top)]
    cand = [[v1[a] + v2[b] for b in range(lens[a])] for a in range(k_top)]
    top = cand[0]
    a = 1
    while lens[a] > 1:
        top = _merge_top(top, cand[a])
        a += 1
    top = _merge_top(top, [cand[i][0] for i in range(a, k_top)])
    thr = top[k_top - 1]

    m = cand[0][0]
    z = None
    for row in cand:
        for c in row:
            e = jnp.where(c >= thr, jnp.exp(c - m), 0.0)
            z = e if z is None else z + e
    inv_z = 1.0 / z

    n_big = 3
    big = []
    for a in range(n_big):
        na = jnp.zeros_like(thr)
        for c in cand[a]:
            na = na + jnp.where(c >= thr, 1.0, 0.0)
        big.append(na)

    inf = jnp.full(thr.shape, jnp.inf, F32)
    tb = []
    for b in range(lens[n_big]):
        t = inf
        for a in range(k_top):
            if b < lens[a]:
                t = jnp.minimum(t, jnp.where(cand[a][b] >= thr, v1[a], inf))
        tb.append(t)

    def bf16_bits(x):
        return pltpu.bitcast(x.astype(BF16).astype(F32), jnp.uint32)

    def pack2(lo, hi):
        return pltpu.bitcast((bf16_bits(lo) >> 16) | bf16_bits(hi), F32)

    for kk in range(nk):
        rows = key_rows(kk)
        s1 = sc_ref[0, rows, :]
        cnt = jnp.zeros_like(s1)
        for b in range(lens[n_big]):
            cnt = jnp.where(s1 >= tb[b], float(b + 1), cnt)
        for a in range(n_big - 1, -1, -1):
            cnt = jnp.where(s1 == v1[a], big[a], cnt)
        out_ref[0, rows, :] = pack2(cnt, cnt)
        g1 = jnp.exp(s1 - v1[0]) * inv_z
        out_ref[1, rows, :] = pack2(g1, g1)

    def second(kk):
        s2 = sc_ref[1, key_rows(kk), :]
        rank = jnp.full(s2.shape, float(k_top), F32)
        for b in range(k_top - 1, -1, -1):
            rank = jnp.where(s2 >= v2[b], float(b), rank)
        return rank, jnp.exp(s2 - v2[0])

    for kp in range(nk // 2):
        r_lo, g_lo = second(2 * kp)
        r_hi, g_hi = second(2 * kp + 1)
        out_ref[2, key_rows(kp), :] = pack2(r_lo, r_hi)
        out_ref[3, key_rows(kp), :] = pack2(g_lo, g_hi)

    for j in range(nsub):
        cols = slice(j * LANES, (j + 1) * LANES)
        n1_ref[:, cols] = out_ref[0, pl.ds(j * pitch, nk), :]
        e1_ref[:, cols] = out_ref[1, pl.ds(j * pitch, nk), :]
        r2_ref[:, cols] = out_ref[2, pl.ds(j * pitch, nk // 2), :]
        e2_ref[:, cols] = out_ref[3, pl.ds(j * pitch, nk // 2), :]


def _route(qp, keys):
    n = qp.shape[0]
    tt = min(SUBLANES * LANES, n)
    nsub = tt // LANES
    pitch = PEER_NKEYS + SUBLANES
    rows = nsub * pitch
    hk = 2 * LANES
    nk, nk2 = PEER_NKEYS, PEER_NKEYS // 2
    blk = lambda r: pl.BlockSpec((None, r, tt), lambda i, h: (h, 0, i))
    shp = lambda r: jax.ShapeDtypeStruct((PEER_HEADS, r, n), F32)
    return pl.pallas_call(
        functools.partial(_route_kernel, pitch=pitch),
        grid=(n // tt, PEER_HEADS),
        in_specs=[pl.BlockSpec((tt, hk), lambda i, h: (i, h)),
                  pl.BlockSpec((2, None, PEER_NKEYS, LANES), lambda i, h: (0, h, 0, 0))],
        out_specs=[blk(nk), blk(nk), blk(nk2), blk(nk2)],
        out_shape=[shp(nk), shp(nk), shp(nk2), shp(nk2)],
        scratch_shapes=[pltpu.VMEM((2, rows, LANES), F32), pltpu.VMEM((4, rows, LANES), F32)],
        compiler_params=_params("parallel", "parallel"),
        name="peer_route",
    )(qp, keys)


def _peer_kernel(xn_ref, x1_ref, n1_ref, e1_ref, r2_ref, e2_ref, u_ref, vt_ref, fn_ref,
                 y_ref, a_ref, hid_ref, acc_ref, *, tcol):
    e = pl.program_id(1)
    tt = xn_ref.shape[0]
    rows_per_tile = u_ref.shape[0] // PEER_NKEYS

    @pl.when(e == 0)
    def _():
        acc_ref[...] = jnp.zeros(acc_ref.shape, F32)

    a_ref[...] = lax.dot_general(u_ref[...], xn_ref[...], (((1,), (1,)), ((), ())),
                                 preferred_element_type=F32).astype(BF16)

    def bcast_row(ref, h, row, cols):
        w = jnp.broadcast_to(ref[h, row:row + 1, cols], (SUBLANES, tcol))
        return jnp.concatenate([pltpu.bitcast(w, BF16)] * (PEER_NKEYS // (2 * SUBLANES)), axis=0)

    def keys_tile(ref, h, cols):
        return pltpu.bitcast(ref[h, :, cols], BF16)

    for i1 in range(rows_per_tile):
        for c in range(tt // tcol):
            cols = slice(c * tcol, (c + 1) * tcol)
            g = None
            for h in range(PEER_HEADS):
                sel = (jnp.where(keys_tile(r2_ref, h, cols) < bcast_row(n1_ref, h, i1, cols),
                                 keys_tile(e2_ref, h, cols), jnp.zeros((), BF16))
                       * bcast_row(e1_ref, h, i1, cols))
                g = sel if g is None else g + sel
            a = a_ref[i1 * PEER_NKEYS:(i1 + 1) * PEER_NKEYS, cols]
            hid_ref[i1 * PEER_NKEYS:(i1 + 1) * PEER_NKEYS, cols] = _gelu(a) * g

    acc_ref[...] += jnp.dot(vt_ref[...], hid_ref[...], preferred_element_type=F32)

    @pl.when(e == pl.num_programs(1) - 1)
    def _():
        x2 = x1_ref[...] + acc_ref[...].T
        y_ref[...] = _rms(x2, fn_ref[...])


def _peer(xn, x1, n1, e1, r2, e2, w, tt=1024, te=1024, tcol=1024):
    n, d = xn.shape
    tt = min(tt, n)
    tcol = min(tcol, tt)
    ne = w['expert_u'].shape[0]
    rt = lambda: pl.BlockSpec((PEER_HEADS, PEER_NKEYS // 2, tt), lambda i, e: (0, 0, i))
    r1 = lambda: pl.BlockSpec((PEER_HEADS, te // PEER_NKEYS, tt), lambda i, e: (0, e, i))
    return pl.pallas_call(
        functools.partial(_peer_kernel, tcol=tcol),
        grid=(n // tt, ne // te),
        in_specs=[pl.BlockSpec((tt, d), lambda i, e: (i, 0)),
                  pl.BlockSpec((tt, d), lambda i, e: (i, 0)),
                  r1(), r1(), rt(), rt(),
                  pl.BlockSpec((te, d), lambda i, e: (e, 0)),
                  pl.BlockSpec((d, te), lambda i, e: (0, e)),
                  pl.BlockSpec((1, d), lambda i, e: (0, 0))],
        out_specs=pl.BlockSpec((tt, d), lambda i, e: (i, 0)),
        out_shape=jax.ShapeDtypeStruct((n, d), F32),
        scratch_shapes=[pltpu.VMEM((te, tt), BF16), pltpu.VMEM((te, tt), BF16),
                        pltpu.VMEM((d, tt), F32)],
        compiler_params=_params("parallel", "arbitrary"),
        name="peer_experts",
    )(xn, x1, n1, e1, r2, e2, w['expert_u'], w['expert_vt'], w['final_norm'])


def _rot_cols(wm):
    hw = wm.shape[-1] // 2
    return jnp.concatenate([-wm[..., hw:], wm[..., :hw]], axis=-1)


def _s5_direction_params(lam_re, lam_im, log_dt, b_re, b_im, c_re, c_im):
    g, p, ch = SSM_GROUPS, SSM_STATE, SSM_GROUP_CH
    dt = jnp.exp(log_dt)[:, None]
    mag = jnp.exp(lam_re * dt)
    a_re = mag * jnp.cos(lam_im * dt)
    a_im = mag * jnp.sin(lam_im * dt)
    nr = a_re - 1.0
    den = lam_re * lam_re + lam_im * lam_im
    f_re = (nr * lam_re + a_im * lam_im) / den
    f_im = (a_im * lam_re - nr * lam_im) / den
    bb_re = f_re[..., None] * b_re - f_im[..., None] * b_im
    bb_im = f_re[..., None] * b_im + f_im[..., None] * b_re
    eye = jnp.eye(g, dtype=F32)
    per_blk = SSM_CHUNKS * LANES // SSM_WIDTH

    def expand(bb):
        nb = SSM_WIDTH // LANES
        m = jnp.einsum('gph,gk->ghkp', bb, eye).reshape(nb, LANES, nb, per_blk, LANES)
        return jnp.einsum('arack->acrk', m).reshape(SSM_CHUNKS, LANES, LANES)

    def readout(cm):
        nb = SSM_WIDTH // LANES
        m = jnp.einsum('ghp,gk->gpkh', cm, eye).reshape(nb, per_blk, LANES, nb, LANES)
        return jnp.einsum('acrak->acrk', m).reshape(SSM_CHUNKS, LANES, LANES)

    wexp = jnp.concatenate([expand(bb_re), expand(bb_im)], axis=-1).astype(BF16)
    cmat = jnp.concatenate([readout(c_re), readout(-c_im)], axis=1).astype(BF16)
    a = jnp.concatenate([a_re.reshape(2, SUBLANES, LANES), a_im.reshape(2, SUBLANES, LANES)])
    return wexp, a, cmat


def _prepare(mix_norm, w_in, q_norm, kv_norm, w_uq, w_ukv, lam_re, lam_im, log_dt, b_re, b_im,
             c_re, c_im, d_skip, w_glu, b_glu, w_out, ffn_norm, w_query, sub_keys, expert_u,
             expert_v, final_norm):
    o1 = Q_LORA
    o2 = o1 + KV_LORA
    o3 = o2 + QK_ROPE
    wi = w_in[0]
    w_kpe = wi[:, o2:o3]
    w = {}
    w['mix_norm'] = mix_norm[0][None, :]
    w['w_in'] = jnp.concatenate([wi[:, :o2], w_kpe, _rot_cols(w_kpe), wi[:, o3:]], axis=1).astype(BF16)
    w['q_norm'] = q_norm[0][None, :]
    w['kv_norm'] = kv_norm[0][None, :]
    wq = w_uq[0]
    wq_pe = wq[..., QK_NOPE:]
    w['w_q'] = jnp.concatenate([wq[..., :QK_NOPE], wq_pe, _rot_cols(wq_pe)], axis=-1).reshape(
        Q_LORA, N_HEADS * QK_PAD).astype(BF16)
    w['w_kv'] = w_ukv[0].reshape(KV_LORA, N_HEADS * (QK_NOPE + V_HEAD)).astype(BF16)
    for name, d in (('f', 0), ('b', 1)):
        wexp, a, cmat = _s5_direction_params(lam_re[0, d], lam_im[0, d], log_dt[0, d], b_re[0, d],
                                             b_im[0, d], c_re[0, d], c_im[0, d])
        w['s5_w' + name], w['s5_a' + name], w['s5_c' + name] = wexp, a, cmat
    w['d_skip'] = d_skip[0].reshape(1, SSM_WIDTH)
    w['w_glu'] = w_glu[0].astype(BF16)
    w['b_glu'] = b_glu[0][None, :]
    mla_w = N_HEADS * V_HEAD
    w['w_out_att'] = w_out[0][:mla_w].astype(BF16)
    w['w_out_ssm'] = w_out[0][mla_w:].astype(BF16)
    w['ffn_norm'] = ffn_norm[0][None, :]
    w['w_query'] = w_query[0].astype(BF16)
    w['sub_keys'] = sub_keys[0].astype(BF16)
    w['expert_u'] = expert_u[0].astype(BF16)
    w['expert_vt'] = expert_v[0].astype(BF16).T
    w['final_norm'] = final_norm[None, :]
    return w


def _rope_table(s):
    inv = ROPE_THETA ** (-jnp.arange(0, QK_ROPE, 2, dtype=F32) / QK_ROPE)
    ang = jnp.arange(s, dtype=F32)[:, None] * inv[None, :]
    c, sn = jnp.cos(ang), jnp.sin(ang)
    return jnp.concatenate([c, c, sn, sn], axis=1)


def _encoder(x, w):
    bsz, s, d = x.shape
    q, k, v, u = _in_stage(x, _rope_table(s), w)
    att = _attention(q, k, v)
    yf, yb = _s5_scan(u, w)
    n = bsz * s
    x1, xn, qp = _mid_stage(x.reshape(n, d), att.reshape(n, -1), yf.reshape(n, -1),
                            yb.reshape(n, -1), u.reshape(n, -1), w)
    n1, e1, r2, e2 = _route(qp, w['sub_keys'])
    y = _peer(xn, x1, n1, e1, r2, e2, w)
    return y.reshape(bsz, s, d)


def kernel(x_prompt, x_sample, mix_norm, w_in, q_norm, kv_norm, w_uq, w_ukv, lam_re, lam_im, log_dt, b_re, b_im, c_re, c_im, d_skip, w_glu, b_glu, w_out, ffn_norm, w_query, sub_keys, expert_u, expert_v, final_norm):
    w = _prepare(mix_norm, w_in, q_norm, kv_norm, w_uq, w_ukv, lam_re, lam_im, log_dt, b_re, b_im,
                 c_re, c_im, d_skip, w_glu, b_glu, w_out, ffn_norm, w_query, sub_keys, expert_u,
                 expert_v, final_norm)
    return (_encoder(x_prompt, w), _encoder(x_sample, w))
```

```python
import functools
import math

import jax
import jax.numpy as jnp
from jax import lax
from jax.experimental import pallas as pl
from jax.experimental.pallas import tpu as pltpu

F32 = jnp.float32
BF16 = jnp.bfloat16

EPS = 1e-6
ROPE_THETA = 10000.0
LANES = 128
SUBLANES = 8
VMEM_LIMIT = 56 * 1024 * 1024

N_HEADS = 4
QK_NOPE = 128
QK_ROPE = 64
V_HEAD = 128
QK_PAD = 256
Q_LORA = 384
KV_LORA = 256
SSM_WIDTH = 512
SSM_GROUPS = 32
SSM_GROUP_CH = 16
SSM_STATE = 64
SSM_CHUNKS = SSM_GROUPS * SSM_STATE // LANES
PEER_HEADS = 8
PEER_NKEYS = 128
PEER_TOPK = 16
GELU_C = math.sqrt(2.0 / math.pi)


def _gelu(x):
    return 0.5 * x * (1.0 + jnp.tanh(GELU_C * (x + 0.044715 * (x * x * x))))


def _rms(x, g):
    return x * lax.rsqrt(jnp.mean(x * x, axis=-1, keepdims=True) + EPS) * g


def _params(*sem):
    return pltpu.CompilerParams(dimension_semantics=sem, vmem_limit_bytes=VMEM_LIMIT)


def _in_kernel(x_ref, cs_ref, g_ref, win_ref, qn_ref, kvn_ref, wq_ref, wkv_ref,
               q_ref, k_ref, v_ref, u_ref):
    x = x_ref[...]
    h = _rms(x, g_ref[...])
    proj = jnp.dot(h.astype(BF16), win_ref[...], preferred_element_type=F32)
    o1 = Q_LORA
    o2 = o1 + KV_LORA
    o3 = o2 + 2 * QK_ROPE
    cq = proj[:, :o1]
    ckv = proj[:, o1:o2]
    kp = proj[:, o2:o3]
    u_ref[...] = proj[:, o3:]
    cs = cs_ref[...]
    lane = lax.broadcasted_iota(jnp.int32, kp.shape, 1)
    t = kp * cs
    kpe = jnp.where(lane < QK_ROPE, t + pltpu.roll(t, QK_ROPE, axis=1), 0.0)
    scale = (QK_NOPE + QK_ROPE) ** -0.5
    q = jnp.dot(_rms(cq, qn_ref[...]).astype(BF16), wq_ref[...], preferred_element_type=F32) * scale
    kv = jnp.dot(_rms(ckv, kvn_ref[...]).astype(BF16), wkv_ref[...], preferred_element_type=F32)
    for hd in range(N_HEADS):
        b = hd * QK_PAD
        tq = q[:, b + QK_NOPE:b + QK_PAD] * cs
        q_ref[hd, :, :QK_NOPE] = q[:, b:b + QK_NOPE].astype(BF16)
        q_ref[hd, :, QK_NOPE:] = (tq + pltpu.roll(tq, QK_ROPE, axis=1)).astype(BF16)
        k_ref[hd, :, :QK_NOPE] = kv[:, b:b + QK_NOPE].astype(BF16)
        k_ref[hd, :, QK_NOPE:] = kpe.astype(BF16)
        v_ref[hd] = kv[:, b + QK_NOPE:b + QK_PAD].astype(BF16)


def _in_stage(x, cs, w, tt=512):
    bsz, s, d = x.shape
    tt = min(tt, s)
    grid = (bsz, s // tt)
    full = lambda a: pl.BlockSpec(a.shape, lambda b, i: (0,) * a.ndim)
    hs = lambda wd: pl.BlockSpec((None, N_HEADS, tt, wd), lambda b, i: (b, 0, i, 0))
    return pl.pallas_call(
        _in_kernel,
        grid=grid,
        in_specs=[pl.BlockSpec((None, tt, d), lambda b, i: (b, i, 0)),
                  pl.BlockSpec((tt, 2 * QK_ROPE), lambda b, i: (i, 0)),
                  full(w['mix_norm']), full(w['w_in']), full(w['q_norm']), full(w['kv_norm']),
                  full(w['w_q']), full(w['w_kv'])],
        out_specs=[hs(QK_PAD), hs(QK_PAD), hs(V_HEAD),
                   pl.BlockSpec((None, tt, SSM_WIDTH), lambda b, i: (b, i, 0))],
        out_shape=[jax.ShapeDtypeStruct((bsz, N_HEADS, s, QK_PAD), BF16),
                   jax.ShapeDtypeStruct((bsz, N_HEADS, s, QK_PAD), BF16),
                   jax.ShapeDtypeStruct((bsz, N_HEADS, s, V_HEAD), BF16),
                   jax.ShapeDtypeStruct((bsz, s, SSM_WIDTH), F32)],
        compiler_params=_params("parallel", "parallel"),
        name="in_stage",
    )(x, cs, w['mix_norm'], w['w_in'], w['q_norm'], w['kv_norm'], w['w_q'], w['w_kv'])


def _flash_kernel(q_ref, k_ref, v_ref, o_ref, m_ref, acc_ref, s0, s1, p0, p1, al0, al1, *, tk):
    n = k_ref.shape[0] // tk
    m_ref[...] = jnp.full(m_ref.shape, -jnp.inf, F32)
    acc_ref[...] = jnp.zeros(acc_ref.shape, F32)
    ones = jnp.ones((tk, LANES), BF16)
    s_buf, p_buf, al_buf = (s0, s1), (p0, p1), (al0, al1)

    def scores(j, slot):
        kj = k_ref[pl.ds(pl.multiple_of(j * tk, tk), tk), :]
        s_buf[slot][...] = lax.dot_general(q_ref[...], kj, (((1,), (1,)), ((), ())),
                                           preferred_element_type=F32)

    def exps(slot):
        s = s_buf[slot][...]
        m_prev = m_ref[...]
        m_new = jnp.maximum(m_prev, jnp.max(s, axis=1, keepdims=True))
        p_buf[slot][...] = jnp.exp(s - jnp.concatenate([m_new] * (tk // LANES), axis=1)).astype(BF16)
        al_buf[slot][...] = jnp.exp(m_prev - m_new)
        m_ref[...] = m_new

    def values(j, slot):
        vj = jnp.concatenate([v_ref[pl.ds(pl.multiple_of(j * tk, tk), tk), :], ones], axis=1)
        acc_ref[...] = (jnp.concatenate([al_buf[slot][...]] * 2, axis=1) * acc_ref[...]
                        + jnp.dot(p_buf[slot][...], vj, preferred_element_type=F32))

    scores(0, 0)
    scores(1, 1)
    exps(0)

    def body(jj, carry):
        j = 2 * jj + 2
        scores(j, 0)
        exps(1)
        values(j - 2, 0)
        scores(j + 1, 1)
        exps(0)
        values(j - 1, 1)
        return carry

    lax.fori_loop(0, (n - 2) // 2, body, 0, unroll=True)
    exps(1)
    values(n - 2, 0)
    values(n - 1, 1)
    o_ref[...] = (acc_ref[:, :V_HEAD] / acc_ref[:, V_HEAD:]).astype(o_ref.dtype)


def _attention(q, k, v, tq=1024, tk=2048):
    bsz, nh, s, _ = q.shape
    tq = min(tq, s)
    tk = min(tk, s // 2)
    assert s % (2 * tk) == 0 and s % tq == 0
    return pl.pallas_call(
        functools.partial(_flash_kernel, tk=tk),
        grid=(bsz, nh, s // tq),
        in_specs=[pl.BlockSpec((None, None, tq, QK_PAD), lambda b, h, i: (b, h, i, 0)),
                  pl.BlockSpec((None, None, s, QK_PAD), lambda b, h, i: (b, h, 0, 0)),
                  pl.BlockSpec((None, None, s, V_HEAD), lambda b, h, i: (b, h, 0, 0))],
        out_specs=pl.BlockSpec((None, tq, V_HEAD), lambda b, h, i: (b, i, h)),
        out_shape=jax.ShapeDtypeStruct((bsz, s, nh * V_HEAD), BF16),
        scratch_shapes=[pltpu.VMEM((tq, LANES), F32), pltpu.VMEM((tq, 2 * V_HEAD), F32),
                        pltpu.VMEM((tq, tk), F32), pltpu.VMEM((tq, tk), F32),
                        pltpu.VMEM((tq, tk), BF16), pltpu.VMEM((tq, tk), BF16),
                        pltpu.VMEM((tq, LANES), F32), pltpu.VMEM((tq, LANES), F32)],
        compiler_params=_params("parallel", "parallel", "arbitrary"),
        name="attention",
    )(q, k, v)


def _s5_kernel(uf_ref, ub_ref, wf_ref, wb_ref, af_ref, ab_ref, cf_ref, cb_ref,
               yf_ref, yb_ref, fre, fim, bre, bim, sfre, sfim, sbre, sbim, st_ref, *, tc, pitch):
    half = SSM_CHUNKS // 2 * pitch

    @pl.when(pl.program_id(1) == 0)
    def _():
        st_ref[...] = jnp.zeros(st_ref.shape, F32)

    for u_ref, w_ref, re, im in ((uf_ref, wf_ref, fre, fim), (ub_ref, wb_ref, bre, bim)):
        for blk in range(SSM_WIDTH // LANES):
            ub = u_ref[:, blk * LANES:(blk + 1) * LANES].astype(BF16)
            for cc in range(SSM_CHUNKS * LANES // SSM_WIDTH):
                ch = blk * (SSM_CHUNKS * LANES // SSM_WIDTH) + cc
                r = jnp.dot(ub, w_ref[ch], preferred_element_type=F32)
                re[pl.ds(ch * pitch, tc), :] = r[:, :LANES]
                im[pl.ds(ch * pitch, tc), :] = r[:, LANES:]

    af = af_ref[...]
    ab = ab_ref[...]

    def step(a, st, re, im, sre, sim, t):
        out = []
        for hf in range(2):
            rows = pl.ds(hf * half + t, SUBLANES, stride=pitch)
            a_re, a_im = a[hf], a[2 + hf]
            s_re, s_im = st[hf], st[2 + hf]
            n_re = a_re * s_re - a_im * s_im + re[rows, :]
            n_im = a_re * s_im + a_im * s_re + im[rows, :]
            sre[rows, :] = n_re
            sim[rows, :] = n_im
            out.append((n_re, n_im))
        return (out[0][0], out[1][0], out[0][1], out[1][1])

    def body(t, carry):
        sf, sb = carry
        sf = step(af, sf, fre, fim, sfre, sfim, t)
        sb = step(ab, sb, bre, bim, sbre, sbim, tc - 1 - t)
        return (sf, sb)

    st = st_ref[...]
    init = (tuple(st[0, i] for i in range(4)), tuple(st[1, i] for i in range(4)))
    sf, sb = lax.fori_loop(0, tc, body, init, unroll=8)
    for i in range(4):
        st_ref[0, i] = sf[i]
        st_ref[1, i] = sb[i]

    for y_ref, c_ref, re, im in ((yf_ref, cf_ref, sfre, sfim), (yb_ref, cb_ref, sbre, sbim)):
        for blk in range(SSM_WIDTH // LANES):
            acc = None
            for cc in range(SSM_CHUNKS * LANES // SSM_WIDTH):
                ch = blk * (SSM_CHUNKS * LANES // SSM_WIDTH) + cc
                s = jnp.concatenate([re[pl.ds(ch * pitch, tc), :].astype(BF16),
                                     im[pl.ds(ch * pitch, tc), :].astype(BF16)], axis=1)
                d = jnp.dot(s, c_ref[ch], preferred_element_type=F32)
                acc = d if acc is None else acc + d
            y_ref[:, blk * LANES:(blk + 1) * LANES] = acc


def _s5_scan(u, w, tc=512):
    bsz, s, _ = u.shape
    tc = min(tc, s)
    n = s // tc
    pitch = tc + SUBLANES // 2
    rows = SSM_CHUNKS * pitch
    full = lambda a: pl.BlockSpec(a.shape, lambda b, c: (0,) * a.ndim)
    ublk = lambda f: pl.BlockSpec((None, tc, SSM_WIDTH), f)
    fwd = lambda b, c: (b, c, 0)
    bwd = lambda b, c: (b, n - 1 - c, 0)
    return pl.pallas_call(
        functools.partial(_s5_kernel, tc=tc, pitch=pitch),
        grid=(bsz, n),
        in_specs=[ublk(fwd), ublk(bwd), full(w['s5_wf']), full(w['s5_wb']),
                  full(w['s5_af']), full(w['s5_ab']), full(w['s5_cf']), full(w['s5_cb'])],
        out_specs=[ublk(fwd), ublk(bwd)],
        out_shape=[jax.ShapeDtypeStruct(u.shape, F32), jax.ShapeDtypeStruct(u.shape, F32)],
        scratch_shapes=[pltpu.VMEM((rows, LANES), F32) for _ in range(8)]
                       + [pltpu.VMEM((2, 4, SUBLANES, LANES), F32)],
        compiler_params=_params("parallel", "arbitrary"),
        name="s5_scan",
    )(u, u, w['s5_wf'], w['s5_wb'], w['s5_af'], w['s5_ab'], w['s5_cf'], w['s5_cb'])


def _mid_kernel(x_ref, att_ref, yf_ref, yb_ref, u_ref, dsk_ref, wglu_ref, bglu_ref,
                woa_ref, wos_ref, fn_ref, wqry_ref, x1_ref, xn_ref, qp_ref):
    y = dsk_ref[...] * u_ref[...] + yf_ref[...] + yb_ref[...]
    z = _gelu(y)
    gate = jnp.dot(z.astype(BF16), wglu_ref[...], preferred_element_type=F32) + bglu_ref[...]
    z = z * (1.0 / (1.0 + jnp.exp(-gate)))
    mix = (jnp.dot(att_ref[...], woa_ref[...], preferred_element_type=F32)
           + jnp.dot(z.astype(BF16), wos_ref[...], preferred_element_type=F32))
    x1 = x_ref[...] + mix
    x1_ref[...] = x1
    xn = _rms(x1, fn_ref[...]).astype(BF16)
    xn_ref[...] = xn
    qp_ref[...] = jnp.dot(xn, wqry_ref[...], preferred_element_type=F32).astype(BF16)


def _mid_stage(x, att, yf, yb, u, w, tt=512):
    n, d = x.shape
    tt = min(tt, n)
    full = lambda a: pl.BlockSpec(a.shape, lambda i: (0,) * a.ndim)
    row = lambda wd: pl.BlockSpec((tt, wd), lambda i: (i, 0))
    nq = w['w_query'].shape[1]
    return pl.pallas_call(
        _mid_kernel,
        grid=(n // tt,),
        in_specs=[row(d), row(att.shape[1]), row(SSM_WIDTH), row(SSM_WIDTH), row(SSM_WIDTH),
                  full(w['d_skip']), full(w['w_glu']), full(w['b_glu']), full(w['w_out_att']),
                  full(w['w_out_ssm']), full(w['ffn_norm']), full(w['w_query'])],
        out_specs=[row(d), row(d), row(nq)],
        out_shape=[jax.ShapeDtypeStruct((n, d), F32), jax.ShapeDtypeStruct((n, d), BF16),
                   jax.ShapeDtypeStruct((n, nq), BF16)],
        compiler_params=_params("parallel"),
        name="mid_stage",
    )(x, att, yf, yb, u, w['d_skip'], w['w_glu'], w['b_glu'], w['w_out_att'], w['w_out_ssm'],
      w['ffn_norm'], w['w_query'])


def _merge_desc(x):
    n = len(x)
    if n == 1:
        return x
    h = n // 2
    hi = [jnp.maximum(x[i], x[i + h]) for i in range(h)]
    lo = [jnp.minimum(x[i], x[i + h]) for i in range(h)]
    return _merge_desc(hi) + _merge_desc(lo)


def _sort_desc(x):
    n = len(x)
    if n == 1:
        return x
    h = n // 2
    return _merge_desc(_sort_desc(x[:h]) + _sort_desc(x[h:])[::-1])


def _merge_top(a, b):
    k = len(a)
    c = [jnp.maximum(a[i], b[k - 1 - i]) if k - 1 - i < len(b) else a[i] for i in range(k)]
    return _merge_desc(c)


def _route_kernel(qp_ref, keys_ref, n1_ref, e1_ref, r2_ref, e2_ref, sc_ref, out_ref, *, pitch):
    k_top = PEER_TOPK
    nk = PEER_NKEYS
    nsub = qp_ref.shape[0] // LANES

    for side in range(2):
        kmat = keys_ref[side]
        for j in range(nsub):
            qj = qp_ref[j * LANES:(j + 1) * LANES, side * LANES:(side + 1) * LANES]
            sc_ref[side, pl.ds(j * pitch, nk), :] = lax.dot_general(
                kmat, qj, (((1,), (1,)), ((), ())), preferred_element_type=F32)

    def key_rows(kk):
        return pl.ds(kk, nsub, stride=pitch)

    def top_sorted(side):
        best = None
        for g in range(nk // k_top):
            grp = _sort_desc([sc_ref[side, key_rows(g * k_top + i), :] for i in range(k_top)])
            best = grp if best is None else _merge_top(best, grp)
        return best

    v1 = top_sorted(0)
    v2 = top_sorted(1)

    lens = [k_top // (a + 1) for a in range(k_top)]
    cand = [[v1[a] + v2[b] for b in range(lens[a])] for a in range(k_top)]
    top = cand[0]
    a = 1
    while lens[a] > 1:
        top = _merge_top(top, cand[a])
        a += 1
    top = _merge_top(top, [cand[i][0] for i in range(a, k_top)])
    thr = top[k_top - 1]

    m = cand[0][0]
    z = None
    for row in cand:
        for c in row:
            e = jnp.where(c >= thr, jnp.exp(c - m), 0.0)
            z = e if z is None else z + e
    inv_z = 1.0 / z

    n_big = 3
    big = []
    for a in range(n_big):
        na = jnp.zeros_like(thr)
        for c in cand[a]:
            na = na + jnp.where(c >= thr, 1.0, 0.0)
        big.append(na)

    inf = jnp.full(thr.shape, jnp.inf, F32)
    tb = []
    for b in range(lens[n_big]):
        t = inf
        for a in range(k_top):
            if b < lens[a]:
                t = jnp.minimum(t, jnp.where(cand[a][b] >= thr, v1[a], inf))
        tb.append(t)

    def bf16_bits(x):
        return pltpu.bitcast(x.astype(BF16).astype(F32), jnp.uint32)

    def pack2(lo, hi):
        return pltpu.bitcast((bf16_bits(lo) >> 16) | bf16_bits(hi), F32)

    for kk in range(nk):
        rows = key_rows(kk)
        s1 = sc_ref[0, rows, :]
        cnt = jnp.zeros_like(s1)
        for b in range(lens[n_big]):
            cnt = jnp.where(s1 >= tb[b], float(b + 1), cnt)
        for a in range(n_big - 1, -1, -1):
            cnt = jnp.where(s1 == v1[a], big[a], cnt)
        out_ref[0, rows, :] = pack2(cnt, cnt)
        g1 = jnp.exp(s1 - v1[0]) * inv_z
        out_ref[1, rows, :] = pack2(g1, g1)

    def second(kk):
        s2 = sc_ref[1, key_rows(kk), :]
        rank = jnp.full(s2.shape, float(k_top), F32)
        for b in range(k_top - 1, -1, -1):
            rank = jnp.where(s2 >= v2[b], float(b), rank)
        return rank, jnp.exp(s2 - v2[0])

    for kp in range(nk // 2):
        r_lo, g_lo = second(2 * kp)
        r_hi, g_hi = second(2 * kp + 1)
        out_ref[2, key_rows(kp), :] = pack2(r_lo, r_hi)
        out_ref[3, key_rows(kp), :] = pack2(g_lo, g_hi)

    for j in range(nsub):
        cols = slice(j * LANES, (j + 1) * LANES)
        n1_ref[:, cols] = out_ref[0, pl.ds(j * pitch, nk), :]
        e1_ref[:, cols] = out_ref[1, pl.ds(j * pitch, nk), :]
        r2_ref[:, cols] = out_ref[2, pl.ds(j * pitch, nk // 2), :]
        e2_ref[:, cols] = out_ref[3, pl.ds(j * pitch, nk // 2), :]


def _route(qp, keys):
    n = qp.shape[0]
    tt = min(SUBLANES * LANES, n)
    nsub = tt // LANES
    pitch = PEER_NKEYS + SUBLANES
    rows = nsub * pitch
    hk = 2 * LANES
    nk, nk2 = PEER_NKEYS, PEER_NKEYS // 2
    blk = lambda r: pl.BlockSpec((None, r, tt), lambda i, h: (h, 0, i))
    shp = lambda r: jax.ShapeDtypeStruct((PEER_HEADS, r, n), F32)
    return pl.pallas_call(
        functools.partial(_route_kernel, pitch=pitch),
        grid=(n // tt, PEER_HEADS),
        in_specs=[pl.BlockSpec((tt, hk), lambda i, h: (i, h)),
                  pl.BlockSpec((2, None, PEER_NKEYS, LANES), lambda i, h: (0, h, 0, 0))],
        out_specs=[blk(nk), blk(nk), blk(nk2), blk(nk2)],
        out_shape=[shp(nk), shp(nk), shp(nk2), shp(nk2)],
        scratch_shapes=[pltpu.VMEM((2, rows, LANES), F32), pltpu.VMEM((4, rows, LANES), F32)],
        compiler_params=_params("parallel", "parallel"),
        name="peer_route",
    )(qp, keys)


def _peer_kernel(xn_ref, x1_ref, n1_ref, e1_ref, r2_ref, e2_ref, u_ref, vt_ref, fn_ref,
                 y_ref, a_ref, hid_ref, acc_ref, *, tcol):
    e = pl.program_id(1)
    tt = xn_ref.shape[0]
    rows_per_tile = u_ref.shape[0] // PEER_NKEYS

    @pl.when(e == 0)
    def _():
        acc_ref[...] = jnp.zeros(acc_ref.shape, F32)

    def bcast_row(ref, h, row, cols):
        w = jnp.broadcast_to(ref[h, row:row + 1, cols], (SUBLANES, tcol))
        return jnp.concatenate([pltpu.bitcast(w, BF16)] * (PEER_NKEYS // (2 * SUBLANES)), axis=0)

    def keys_tile(ref, h, cols):
        return pltpu.bitcast(ref[h, :, cols], BF16)

    for c in range(tt // tcol):
        cols = slice(c * tcol, (c + 1) * tcol)
        a_ref[:, cols] = lax.dot_general(u_ref[...], xn_ref[cols, :], (((1,), (1,)), ((), ())),
                                         preferred_element_type=F32).astype(BF16)
        for i1 in range(rows_per_tile):
            g = None
            for h in range(PEER_HEADS):
                sel = (jnp.where(keys_tile(r2_ref, h, cols) < bcast_row(n1_ref, h, i1, cols),
                                 keys_tile(e2_ref, h, cols), jnp.zeros((), BF16))
                       * bcast_row(e1_ref, h, i1, cols))
                g = sel if g is None else g + sel
            a = a_ref[i1 * PEER_NKEYS:(i1 + 1) * PEER_NKEYS, cols]
            hid_ref[i1 * PEER_NKEYS:(i1 + 1) * PEER_NKEYS, cols] = _gelu(a) * g
        acc_ref[:, cols] += jnp.dot(vt_ref[...], hid_ref[:, cols], preferred_element_type=F32)

    @pl.when(e == pl.num_programs(1) - 1)
    def _():
        x2 = x1_ref[...] + acc_ref[...].T
        y_ref[...] = _rms(x2, fn_ref[...])


def _peer(xn, x1, n1, e1, r2, e2, w, tt=1024, te=1024, tcol=512):
    n, d = xn.shape
    tt = min(tt, n)
    tcol = min(tcol, tt)
    ne = w['expert_u'].shape[0]
    rt = lambda: pl.BlockSpec((PEER_HEADS, PEER_NKEYS // 2, tt), lambda i, e: (0, 0, i))
    r1 = lambda: pl.BlockSpec((PEER_HEADS, te // PEER_NKEYS, tt), lambda i, e: (0, e, i))
    return pl.pallas_call(
        functools.partial(_peer_kernel, tcol=tcol),
        grid=(n // tt, ne // te),
        in_specs=[pl.BlockSpec((tt, d), lambda i, e: (i, 0)),
                  pl.BlockSpec((tt, d), lambda i, e: (i, 0)),
                  r1(), r1(), rt(), rt(),
                  pl.BlockSpec((te, d), lambda i, e: (e, 0)),
                  pl.BlockSpec((d, te), lambda i, e: (0, e)),
                  pl.BlockSpec((1, d), lambda i, e: (0, 0))],
        out_specs=pl.BlockSpec((tt, d), lambda i, e: (i, 0)),
        out_shape=jax.ShapeDtypeStruct((n, d), F32),
        scratch_shapes=[pltpu.VMEM((te, tt), BF16), pltpu.VMEM((te, tt), BF16),
                        pltpu.VMEM((d, tt), F32)],
        compiler_params=_params("parallel", "arbitrary"),
        name="peer_experts",
    )(xn, x1, n1, e1, r2, e2, w['expert_u'], w['expert_vt'], w['final_norm'])


def _rot_cols(wm):
    hw = wm.shape[-1] // 2
    return jnp.concatenate([-wm[..., hw:], wm[..., :hw]], axis=-1)


def _s5_direction_params(lam_re, lam_im, log_dt, b_re, b_im, c_re, c_im):
    g, p, ch = SSM_GROUPS, SSM_STATE, SSM_GROUP_CH
    dt = jnp.exp(log_dt)[:, None]
    mag = jnp.exp(lam_re * dt)
    a_re = mag * jnp.cos(lam_im * dt)
    a_im = mag * jnp.sin(lam_im * dt)
    nr = a_re - 1.0
    den = lam_re * lam_re + lam_im * lam_im
    f_re = (nr * lam_re + a_im * lam_im) / den
    f_im = (a_im * lam_re - nr * lam_im) / den
    bb_re = f_re[..., None] * b_re - f_im[..., None] * b_im
    bb_im = f_re[..., None] * b_im + f_im[..., None] * b_re
    eye = jnp.eye(g, dtype=F32)
    per_blk = SSM_CHUNKS * LANES // SSM_WIDTH

    def expand(bb):
        nb = SSM_WIDTH // LANES
        m = jnp.einsum('gph,gk->ghkp', bb, eye).reshape(nb, LANES, nb, per_blk, LANES)
        return jnp.einsum('arack->acrk', m).reshape(SSM_CHUNKS, LANES, LANES)

    def readout(cm):
        nb = SSM_WIDTH // LANES
        m = jnp.einsum('ghp,gk->gpkh', cm, eye).reshape(nb, per_blk, LANES, nb, LANES)
        return jnp.einsum('acrak->acrk', m).reshape(SSM_CHUNKS, LANES, LANES)

    wexp = jnp.concatenate([expand(bb_re), expand(bb_im)], axis=-1).astype(BF16)
    cmat = jnp.concatenate([readout(c_re), readout(-c_im)], axis=1).astype(BF16)
    a = jnp.concatenate([a_re.reshape(2, SUBLANES, LANES), a_im.reshape(2, SUBLANES, LANES)])
    return wexp, a, cmat


def _prepare(mix_norm, w_in, q_norm, kv_norm, w_uq, w_ukv, lam_re, lam_im, log_dt, b_re, b_im,
             c_re, c_im, d_skip, w_glu, b_glu, w_out, ffn_norm, w_query, sub_keys, expert_u,
             expert_v, final_norm):
    o1 = Q_LORA
    o2 = o1 + KV_LORA
    o3 = o2 + QK_ROPE
    wi = w_in[0]
    w_kpe = wi[:, o2:o3]
    w = {}
    w['mix_norm'] = mix_norm[0][None, :]
    w['w_in'] = jnp.concatenate([wi[:, :o2], w_kpe, _rot_cols(w_kpe), wi[:, o3:]], axis=1).astype(BF16)
    w['q_norm'] = q_norm[0][None, :]
    w['kv_norm'] = kv_norm[0][None, :]
    wq = w_uq[0]
    wq_pe = wq[..., QK_NOPE:]
    w['w_q'] = jnp.concatenate([wq[..., :QK_NOPE], wq_pe, _rot_cols(wq_pe)], axis=-1).reshape(
        Q_LORA, N_HEADS * QK_PAD).astype(BF16)
    w['w_kv'] = w_ukv[0].reshape(KV_LORA, N_HEADS * (QK_NOPE + V_HEAD)).astype(BF16)
    for name, d in (('f', 0), ('b', 1)):
        wexp, a, cmat = _s5_direction_params(lam_re[0, d], lam_im[0, d], log_dt[0, d], b_re[0, d],
                                             b_im[0, d], c_re[0, d], c_im[0, d])
        w['s5_w' + name], w['s5_a' + name], w['s5_c' + name] = wexp, a, cmat
    w['d_skip'] = d_skip[0].reshape(1, SSM_WIDTH)
    w['w_glu'] = w_glu[0].astype(BF16)
    w['b_glu'] = b_glu[0][None, :]
    mla_w = N_HEADS * V_HEAD
    w['w_out_att'] = w_out[0][:mla_w].astype(BF16)
    w['w_out_ssm'] = w_out[0][mla_w:].astype(BF16)
    w['ffn_norm'] = ffn_norm[0][None, :]
    w['w_query'] = w_query[0].astype(BF16)
    w['sub_keys'] = sub_keys[0].astype(BF16)
    w['expert_u'] = expert_u[0].astype(BF16)
    w['expert_vt'] = expert_v[0].astype(BF16).T
    w['final_norm'] = final_norm[None, :]
    return w


def _rope_table(s):
    inv = ROPE_THETA ** (-jnp.arange(0, QK_ROPE, 2, dtype=F32) / QK_ROPE)
    ang = jnp.arange(s, dtype=F32)[:, None] * inv[None, :]
    c, sn = jnp.cos(ang), jnp.sin(ang)
    return jnp.concatenate([c, c, sn, sn], axis=1)


def _encoder(x, w):
    bsz, s, d = x.shape
    q, k, v, u = _in_stage(x, _rope_table(s), w)
    att = _attention(q, k, v)
    yf, yb = _s5_scan(u, w)
    n = bsz * s
    x1, xn, qp = _mid_stage(x.reshape(n, d), att.reshape(n, -1), yf.reshape(n, -1),
                            yb.reshape(n, -1), u.reshape(n, -1), w)
    n1, e1, r2, e2 = _route(qp, w['sub_keys'])
    y = _peer(xn, x1, n1, e1, r2, e2, w)
    return y.reshape(bsz, s, d)


def kernel(x_prompt, x_sample, mix_norm, w_in, q_norm, kv_norm, w_uq, w_ukv, lam_re, lam_im, log_dt, b_re, b_im, c_re, c_im, d_skip, w_glu, b_glu, w_out, ffn_norm, w_query, sub_keys, expert_u, expert_v, final_norm):
    w = _prepare(mix_norm, w_in, q_norm, kv_norm, w_uq, w_ukv, lam_re, lam_im, log_dt, b_re, b_im,
                 c_re, c_im, d_skip, w_glu, b_glu, w_out, ffn_norm, w_query, sub_keys, expert_u,
                 expert_v, final_norm)
    return (_encoder(x_prompt, w), _encoder(x_sample, w))
```

```python
import functools
import math

import jax
import jax.numpy as jnp
from jax import lax
from jax.experimental import pallas as pl
from jax.experimental.pallas import tpu as pltpu

F32 = jnp.float32
BF16 = jnp.bfloat16

EPS = 1e-6
ROPE_THETA = 10000.0
LANES = 128
SUBLANES = 8
VMEM_LIMIT = 56 * 1024 * 1024

N_HEADS = 4
QK_NOPE = 128
QK_ROPE = 64
V_HEAD = 128
QK_PAD = 256
Q_LORA = 384
KV_LORA = 256
SSM_WIDTH = 512
SSM_GROUPS = 32
SSM_GROUP_CH = 16
SSM_STATE = 64
SSM_CHUNKS = SSM_GROUPS * SSM_STATE // LANES
PEER_HEADS = 8
PEER_NKEYS = 128
PEER_TOPK = 16
GELU_C = math.sqrt(2.0 / math.pi)


def _gelu(x):
    return 0.5 * x * (1.0 + jnp.tanh(GELU_C * (x + 0.044715 * (x * x * x))))


def _rms(x, g):
    return x * lax.rsqrt(jnp.mean(x * x, axis=-1, keepdims=True) + EPS) * g


def _params(*sem):
    return pltpu.CompilerParams(dimension_semantics=sem, vmem_limit_bytes=VMEM_LIMIT)


def _in_kernel(x_ref, cs_ref, g_ref, win_ref, qn_ref, kvn_ref, wq_ref, wkv_ref,
               q_ref, k_ref, v_ref, u_ref):
    x = x_ref[...]
    h = _rms(x, g_ref[...])
    proj = jnp.dot(h.astype(BF16), win_ref[...], preferred_element_type=F32)
    o1 = Q_LORA
    o2 = o1 + KV_LORA
    o3 = o2 + 2 * QK_ROPE
    cq = proj[:, :o1]
    ckv = proj[:, o1:o2]
    kp = proj[:, o2:o3]
    u_ref[...] = proj[:, o3:]
    cs = cs_ref[...]
    lane = lax.broadcasted_iota(jnp.int32, kp.shape, 1)
    t = kp * cs
    kpe = jnp.where(lane < QK_ROPE, t + pltpu.roll(t, QK_ROPE, axis=1), 0.0)
    scale = (QK_NOPE + QK_ROPE) ** -0.5
    q = jnp.dot(_rms(cq, qn_ref[...]).astype(BF16), wq_ref[...], preferred_element_type=F32) * scale
    kv = jnp.dot(_rms(ckv, kvn_ref[...]).astype(BF16), wkv_ref[...], preferred_element_type=F32)
    for hd in range(N_HEADS):
        b = hd * QK_PAD
        tq = q[:, b + QK_NOPE:b + QK_PAD] * cs
        q_ref[hd, :, :QK_NOPE] = q[:, b:b + QK_NOPE].astype(BF16)
        q_ref[hd, :, QK_NOPE:] = (tq + pltpu.roll(tq, QK_ROPE, axis=1)).astype(BF16)
        k_ref[hd, :, :QK_NOPE] = kv[:, b:b + QK_NOPE].astype(BF16)
        k_ref[hd, :, QK_NOPE:] = kpe.astype(BF16)
        v_ref[hd] = kv[:, b + QK_NOPE:b + QK_PAD].astype(BF16)


def _in_stage(x, cs, w, tt=512):
    bsz, s, d = x.shape
    tt = min(tt, s)
    grid = (bsz, s // tt)
    full = lambda a: pl.BlockSpec(a.shape, lambda b, i: (0,) * a.ndim)
    hs = lambda wd: pl.BlockSpec((None, N_HEADS, tt, wd), lambda b, i: (b, 0, i, 0))
    return pl.pallas_call(
        _in_kernel,
        grid=grid,
        in_specs=[pl.BlockSpec((None, tt, d), lambda b, i: (b, i, 0)),
                  pl.BlockSpec((tt, 2 * QK_ROPE), lambda b, i: (i, 0)),
                  full(w['mix_norm']), full(w['w_in']), full(w['q_norm']), full(w['kv_norm']),
                  full(w['w_q']), full(w['w_kv'])],
        out_specs=[hs(QK_PAD), hs(QK_PAD), hs(V_HEAD),
                   pl.BlockSpec((None, tt, SSM_WIDTH), lambda b, i: (b, i, 0))],
        out_shape=[jax.ShapeDtypeStruct((bsz, N_HEADS, s, QK_PAD), BF16),
                   jax.ShapeDtypeStruct((bsz, N_HEADS, s, QK_PAD), BF16),
                   jax.ShapeDtypeStruct((bsz, N_HEADS, s, V_HEAD), BF16),
                   jax.ShapeDtypeStruct((bsz, s, SSM_WIDTH), F32)],
        compiler_params=_params("parallel", "parallel"),
        name="in_stage",
    )(x, cs, w['mix_norm'], w['w_in'], w['q_norm'], w['kv_norm'], w['w_q'], w['w_kv'])


def _flash_kernel(q_ref, k_ref, v_ref, o_ref, m_ref, acc_ref, s0, s1, p0, p1, al0, al1, *, tk):
    n = k_ref.shape[0] // tk
    m_ref[...] = jnp.full(m_ref.shape, -jnp.inf, F32)
    acc_ref[...] = jnp.zeros(acc_ref.shape, F32)
    ones = jnp.ones((tk, LANES), BF16)
    s_buf, p_buf, al_buf = (s0, s1), (p0, p1), (al0, al1)

    def scores(j, slot):
        kj = k_ref[pl.ds(pl.multiple_of(j * tk, tk), tk), :]
        s_buf[slot][...] = lax.dot_general(q_ref[...], kj, (((1,), (1,)), ((), ())),
                                           preferred_element_type=F32)

    def exps(slot):
        s = s_buf[slot][...]
        m_prev = m_ref[...]
        m_new = jnp.maximum(m_prev, jnp.max(s, axis=1, keepdims=True))
        p_buf[slot][...] = jnp.exp(s - jnp.concatenate([m_new] * (tk // LANES), axis=1)).astype(BF16)
        al_buf[slot][...] = jnp.exp(m_prev - m_new)
        m_ref[...] = m_new

    def values(j, slot):
        vj = jnp.concatenate([v_ref[pl.ds(pl.multiple_of(j * tk, tk), tk), :], ones], axis=1)
        acc_ref[...] = (jnp.concatenate([al_buf[slot][...]] * 2, axis=1) * acc_ref[...]
                        + jnp.dot(p_buf[slot][...], vj, preferred_element_type=F32))

    scores(0, 0)
    scores(1, 1)
    exps(0)

    def body(jj, carry):
        j = 2 * jj + 2
        scores(j, 0)
        exps(1)
        values(j - 2, 0)
        scores(j + 1, 1)
        exps(0)
        values(j - 1, 1)
        return carry

    lax.fori_loop(0, (n - 2) // 2, body, 0, unroll=True)
    exps(1)
    values(n - 2, 0)
    values(n - 1, 1)
    o_ref[...] = (acc_ref[:, :V_HEAD] / acc_ref[:, V_HEAD:]).astype(o_ref.dtype)


def _attention(q, k, v, tq=1024, tk=2048):
    bsz, nh, s, _ = q.shape
    tq = min(tq, s)
    tk = min(tk, s // 2)
    assert s % (2 * tk) == 0 and s % tq == 0
    return pl.pallas_call(
        functools.partial(_flash_kernel, tk=tk),
        grid=(bsz, nh, s // tq),
        in_specs=[pl.BlockSpec((None, None, tq, QK_PAD), lambda b, h, i: (b, h, i, 0)),
                  pl.BlockSpec((None, None, s, QK_PAD), lambda b, h, i: (b, h, 0, 0)),
                  pl.BlockSpec((None, None, s, V_HEAD), lambda b, h, i: (b, h, 0, 0))],
        out_specs=pl.BlockSpec((None, tq, V_HEAD), lambda b, h, i: (b, i, h)),
        out_shape=jax.ShapeDtypeStruct((bsz, s, nh * V_HEAD), BF16),
        scratch_shapes=[pltpu.VMEM((tq, LANES), F32), pltpu.VMEM((tq, 2 * V_HEAD), F32),
                        pltpu.VMEM((tq, tk), F32), pltpu.VMEM((tq, tk), F32),
                        pltpu.VMEM((tq, tk), BF16), pltpu.VMEM((tq, tk), BF16),
                        pltpu.VMEM((tq, LANES), F32), pltpu.VMEM((tq, LANES), F32)],
        compiler_params=_params("parallel", "parallel", "arbitrary"),
        name="attention",
    )(q, k, v)


def _s5_kernel(uf_ref, ub_ref, wf_ref, wb_ref, af_ref, ab_ref, cf_ref, cb_ref,
               yf_ref, yb_ref, fre, fim, bre, bim, sfre, sfim, sbre, sbim, st_ref, *, tc, pitch):
    half = SSM_CHUNKS // 2 * pitch

    @pl.when(pl.program_id(1) == 0)
    def _():
        st_ref[...] = jnp.zeros(st_ref.shape, F32)

    for u_ref, w_ref, re, im in ((uf_ref, wf_ref, fre, fim), (ub_ref, wb_ref, bre, bim)):
        for blk in range(SSM_WIDTH // LANES):
            ub = u_ref[:, blk * LANES:(blk + 1) * LANES].astype(BF16)
            for cc in range(SSM_CHUNKS * LANES // SSM_WIDTH):
                ch = blk * (SSM_CHUNKS * LANES // SSM_WIDTH) + cc
                r = jnp.dot(ub, w_ref[ch], preferred_element_type=F32)
                re[pl.ds(ch * pitch, tc), :] = r[:, :LANES]
                im[pl.ds(ch * pitch, tc), :] = r[:, LANES:]

    af = af_ref[...]
    ab = ab_ref[...]

    def step(a, st, re, im, sre, sim, t):
        out = []
        for hf in range(2):
            rows = pl.ds(hf * half + t, SUBLANES, stride=pitch)
            a_re, a_im = a[hf], a[2 + hf]
            s_re, s_im = st[hf], st[2 + hf]
            n_re = a_re * s_re - a_im * s_im + re[rows, :]
            n_im = a_re * s_im + a_im * s_re + im[rows, :]
            sre[rows, :] = n_re
            sim[rows, :] = n_im
            out.append((n_re, n_im))
        return (out[0][0], out[1][0], out[0][1], out[1][1])

    def body(t, carry):
        sf, sb = carry
        sf = step(af, sf, fre, fim, sfre, sfim, t)
        sb = step(ab, sb, bre, bim, sbre, sbim, tc - 1 - t)
        return (sf, sb)

    st = st_ref[...]
    init = (tuple(st[0, i] for i in range(4)), tuple(st[1, i] for i in range(4)))
    sf, sb = lax.fori_loop(0, tc, body, init, unroll=8)
    for i in range(4):
        st_ref[0, i] = sf[i]
        st_ref[1, i] = sb[i]

    for y_ref, c_ref, re, im in ((yf_ref, cf_ref, sfre, sfim), (yb_ref, cb_ref, sbre, sbim)):
        for blk in range(SSM_WIDTH // LANES):
            acc = None
            for cc in range(SSM_CHUNKS * LANES // SSM_WIDTH):
                ch = blk * (SSM_CHUNKS * LANES // SSM_WIDTH) + cc
                s = jnp.concatenate([re[pl.ds(ch * pitch, tc), :].astype(BF16),
                                     im[pl.ds(ch * pitch, tc), :].astype(BF16)], axis=1)
                d = jnp.dot(s, c_ref[ch], preferred_element_type=F32)
                acc = d if acc is None else acc + d
            y_ref[:, blk * LANES:(blk + 1) * LANES] = acc


def _s5_scan(u, w, tc=512):
    bsz, s, _ = u.shape
    tc = min(tc, s)
    n = s // tc
    pitch = tc + SUBLANES // 2
    rows = SSM_CHUNKS * pitch
    full = lambda a: pl.BlockSpec(a.shape, lambda b, c: (0,) * a.ndim)
    ublk = lambda f: pl.BlockSpec((None, tc, SSM_WIDTH), f)
    fwd = lambda b, c: (b, c, 0)
    bwd = lambda b, c: (b, n - 1 - c, 0)
    return pl.pallas_call(
        functools.partial(_s5_kernel, tc=tc, pitch=pitch),
        grid=(bsz, n),
        in_specs=[ublk(fwd), ublk(bwd), full(w['s5_wf']), full(w['s5_wb']),
                  full(w['s5_af']), full(w['s5_ab']), full(w['s5_cf']), full(w['s5_cb'])],
        out_specs=[ublk(fwd), ublk(bwd)],
        out_shape=[jax.ShapeDtypeStruct(u.shape, F32), jax.ShapeDtypeStruct(u.shape, F32)],
        scratch_shapes=[pltpu.VMEM((rows, LANES), F32) for _ in range(8)]
                       + [pltpu.VMEM((2, 4, SUBLANES, LANES), F32)],
        compiler_params=_params("parallel", "arbitrary"),
        name="s5_scan",
    )(u, u, w['s5_wf'], w['s5_wb'], w['s5_af'], w['s5_ab'], w['s5_cf'], w['s5_cb'])


def _mid_kernel(x_ref, att_ref, yf_ref, yb_ref, u_ref, dsk_ref, wglu_ref, bglu_ref,
                woa_ref, wos_ref, fn_ref, wqry_ref, x1_ref, xn_ref, qp_ref):
    y = dsk_ref[...] * u_ref[...] + yf_ref[...] + yb_ref[...]
    z = _gelu(y)
    gate = jnp.dot(z.astype(BF16), wglu_ref[...], preferred_element_type=F32) + bglu_ref[...]
    z = z * (1.0 / (1.0 + jnp.exp(-gate)))
    mix = (jnp.dot(att_ref[...], woa_ref[...], preferred_element_type=F32)
           + jnp.dot(z.astype(BF16), wos_ref[...], preferred_element_type=F32))
    x1 = x_ref[...] + mix
    x1_ref[...] = x1
    xn = _rms(x1, fn_ref[...]).astype(BF16)
    xn_ref[...] = xn
    qp_ref[...] = jnp.dot(xn, wqry_ref[...], preferred_element_type=F32).astype(BF16)


def _mid_stage(x, att, yf, yb, u, w, tt=512):
    n, d = x.shape
    tt = min(tt, n)
    full = lambda a: pl.BlockSpec(a.shape, lambda i: (0,) * a.ndim)
    row = lambda wd: pl.BlockSpec((tt, wd), lambda i: (i, 0))
    nq = w['w_query'].shape[1]
    return pl.pallas_call(
        _mid_kernel,
        grid=(n // tt,),
        in_specs=[row(d), row(att.shape[1]), row(SSM_WIDTH), row(SSM_WIDTH), row(SSM_WIDTH),
                  full(w['d_skip']), full(w['w_glu']), full(w['b_glu']), full(w['w_out_att']),
                  full(w['w_out_ssm']), full(w['ffn_norm']), full(w['w_query'])],
        out_specs=[row(d), row(d), row(nq)],
        out_shape=[jax.ShapeDtypeStruct((n, d), F32), jax.ShapeDtypeStruct((n, d), BF16),
                   jax.ShapeDtypeStruct((n, nq), BF16)],
        compiler_params=_params("parallel"),
        name="mid_stage",
    )(x, att, yf, yb, u, w['d_skip'], w['w_glu'], w['b_glu'], w['w_out_att'], w['w_out_ssm'],
      w['ffn_norm'], w['w_query'])


def _merge_desc(x):
    n = len(x)
    if n == 1:
        return x
    h = n // 2
    hi = [jnp.maximum(x[i], x[i + h]) for i in range(h)]
    lo = [jnp.minimum(x[i], x[i + h]) for i in range(h)]
    return _merge_desc(hi) + _merge_desc(lo)


def _sort_desc(x):
    n = len(x)
    if n == 1:
        return x
    h = n // 2
    return _merge_desc(_sort_desc(x[:h]) + _sort_desc(x[h:])[::-1])


def _merge_top(a, b):
    k = len(a)
    c = [jnp.maximum(a[i], b[k - 1 - i]) if k - 1 - i < len(b) else a[i] for i in range(k)]
    return _merge_desc(c)


def _route_kernel(qp_ref, keys_ref, n1_ref, e1_ref, r2_ref, e2_ref, sc_ref, out_ref, *, pitch, nh):
    hk = 2 * LANES
    for hh in range(nh):
        _route_one(qp_ref.at[:, hh * hk:(hh + 1) * hk], keys_ref.at[:, hh], n1_ref.at[hh],
                   e1_ref.at[hh], r2_ref.at[hh], e2_ref.at[hh], sc_ref.at[hh], out_ref.at[hh],
                   pitch=pitch)


def _route_one(qp_ref, keys_ref, n1_ref, e1_ref, r2_ref, e2_ref, sc_ref, out_ref, *, pitch):
    k_top = PEER_TOPK
    nk = PEER_NKEYS
    nsub = qp_ref.shape[0] // LANES

    for side in range(2):
        kmat = keys_ref[side]
        for j in range(nsub):
            qj = qp_ref[j * LANES:(j + 1) * LANES, side * LANES:(side + 1) * LANES]
            sc_ref[side, pl.ds(j * pitch, nk), :] = lax.dot_general(
                kmat, qj, (((1,), (1,)), ((), ())), preferred_element_type=F32)

    def key_rows(kk):
        return pl.ds(kk, nsub, stride=pitch)

    def top_sorted(side):
        best = None
        for g in range(nk // k_top):
            grp = _sort_desc([sc_ref[side, key_rows(g * k_top + i), :] for i in range(k_top)])
            best = grp if best is None else _merge_top(best, grp)
        return best

    v1 = top_sorted(0)
    v2 = top_sorted(1)

    lens = [k_top // (a + 1) for a in range(k_top)]
    cand = [[v1[a] + v2[b] for b in range(lens[a])] for a in range(k_top)]
    top = cand[0]
    a = 1
    while lens[a] > 1:
        top = _merge_top(top, cand[a])
        a += 1
    top = _merge_top(top, [cand[i][0] for i in range(a, k_top)])
    thr = top[k_top - 1]

    m = cand[0][0]
    z = None
    for row in cand:
        for c in row:
            e = jnp.where(c >= thr, jnp.exp(c - m), 0.0)
            z = e if z is None else z + e
    inv_z = 1.0 / z

    n_big = 3
    big = []
    for a in range(n_big):
        na = jnp.zeros_like(thr)
        for c in cand[a]:
            na = na + jnp.where(c >= thr, 1.0, 0.0)
        big.append(na)

    inf = jnp.full(thr.shape, jnp.inf, F32)
    tb = []
    for b in range(lens[n_big]):
        t = inf
        for a in range(k_top):
            if b < lens[a]:
                t = jnp.minimum(t, jnp.where(cand[a][b] >= thr, v1[a], inf))
        tb.append(t)

    def bf16_bits(x):
        return pltpu.bitcast(x.astype(BF16).astype(F32), jnp.uint32)

    def pack2(lo, hi):
        return pltpu.bitcast((bf16_bits(lo) >> 16) | bf16_bits(hi), F32)

    for kk in range(nk):
        rows = key_rows(kk)
        s1 = sc_ref[0, rows, :]
        cnt = jnp.zeros_like(s1)
        for b in range(lens[n_big]):
            cnt = jnp.where(s1 >= tb[b], float(b + 1), cnt)
        for a in range(n_big - 1, -1, -1):
            cnt = jnp.where(s1 == v1[a], big[a], cnt)
        out_ref[0, rows, :] = pack2(cnt, cnt)
        g1 = jnp.exp(s1 - v1[0]) * inv_z
        out_ref[1, rows, :] = pack2(g1, g1)

    def second(kk):
        s2 = sc_ref[1, key_rows(kk), :]
        rank = jnp.full(s2.shape, float(k_top), F32)
        for b in range(k_top - 1, -1, -1):
            rank = jnp.where(s2 >= v2[b], float(b), rank)
        return rank, jnp.exp(s2 - v2[0])

    for kp in range(nk // 2):
        r_lo, g_lo = second(2 * kp)
        r_hi, g_hi = second(2 * kp + 1)
        out_ref[2, key_rows(kp), :] = pack2(r_lo, r_hi)
        out_ref[3, key_rows(kp), :] = pack2(g_lo, g_hi)

    for j in range(nsub):
        cols = slice(j * LANES, (j + 1) * LANES)
        n1_ref[:, cols] = out_ref[0, pl.ds(j * pitch, nk), :]
        e1_ref[:, cols] = out_ref[1, pl.ds(j * pitch, nk), :]
        r2_ref[:, cols] = out_ref[2, pl.ds(j * pitch, nk // 2), :]
        e2_ref[:, cols] = out_ref[3, pl.ds(j * pitch, nk // 2), :]


def _route(qp, keys):
    n = qp.shape[0]
    tt = min(SUBLANES * LANES, n)
    nsub = tt // LANES
    pitch = PEER_NKEYS + SUBLANES
    rows = nsub * pitch
    hk = 2 * LANES
    nk, nk2 = PEER_NKEYS, PEER_NKEYS // 2
    nh = 2
    blk = lambda r: pl.BlockSpec((nh, r, tt), lambda i, h: (h, 0, i))
    shp = lambda r: jax.ShapeDtypeStruct((PEER_HEADS, r, n), F32)
    return pl.pallas_call(
        functools.partial(_route_kernel, pitch=pitch, nh=nh),
        grid=(n // tt, PEER_HEADS // nh),
        in_specs=[pl.BlockSpec((tt, nh * hk), lambda i, h: (i, h)),
                  pl.BlockSpec((2, nh, PEER_NKEYS, LANES), lambda i, h: (0, h, 0, 0))],
        out_specs=[blk(nk), blk(nk), blk(nk2), blk(nk2)],
        out_shape=[shp(nk), shp(nk), shp(nk2), shp(nk2)],
        scratch_shapes=[pltpu.VMEM((nh, 2, rows, LANES), F32),
                        pltpu.VMEM((nh, 4, rows, LANES), F32)],
        compiler_params=_params("parallel", "parallel"),
        name="peer_route",
    )(qp, keys)


def _peer_kernel(xn_ref, x1_ref, n1_ref, e1_ref, r2_ref, e2_ref, u_ref, vt_ref, fn_ref,
                 y_ref, a_ref, hid_ref, acc_ref, *, tcol):
    e = pl.program_id(1)
    tt = xn_ref.shape[0]
    rows_per_tile = u_ref.shape[0] // PEER_NKEYS

    @pl.when(e == 0)
    def _():
        acc_ref[...] = jnp.zeros(acc_ref.shape, F32)

    a_ref[...] = lax.dot_general(u_ref[...], xn_ref[...], (((1,), (1,)), ((), ())),
                                 preferred_element_type=F32).astype(BF16)

    def bcast_row(ref, h, row, cols):
        w = jnp.broadcast_to(ref[h, row:row + 1, cols], (SUBLANES, tcol))
        return jnp.concatenate([pltpu.bitcast(w, BF16)] * (PEER_NKEYS // (2 * SUBLANES)), axis=0)

    def keys_tile(ref, h, cols):
        return pltpu.bitcast(ref[h, :, cols], BF16)

    for i1 in range(rows_per_tile):
        for c in range(tt // tcol):
            cols = slice(c * tcol, (c + 1) * tcol)
            g = None
            for h in range(PEER_HEADS):
                sel = (jnp.where(keys_tile(r2_ref, h, cols) < bcast_row(n1_ref, h, i1, cols),
                                 keys_tile(e2_ref, h, cols), jnp.zeros((), BF16))
                       * bcast_row(e1_ref, h, i1, cols))
                g = sel if g is None else g + sel
            a = a_ref[i1 * PEER_NKEYS:(i1 + 1) * PEER_NKEYS, cols]
            hid_ref[i1 * PEER_NKEYS:(i1 + 1) * PEER_NKEYS, cols] = _gelu(a) * g

    acc_ref[...] += jnp.dot(vt_ref[...], hid_ref[...], preferred_element_type=F32)

    @pl.when(e == pl.num_programs(1) - 1)
    def _():
        x2 = x1_ref[...] + acc_ref[...].T
        y_ref[...] = _rms(x2, fn_ref[...])


def _peer(xn, x1, n1, e1, r2, e2, w, tt=1024, te=1024, tcol=1024):
    n, d = xn.shape
    tt = min(tt, n)
    tcol = min(tcol, tt)
    ne = w['expert_u'].shape[0]
    rt = lambda: pl.BlockSpec((PEER_HEADS, PEER_NKEYS // 2, tt), lambda i, e: (0, 0, i))
    r1 = lambda: pl.BlockSpec((PEER_HEADS, te // PEER_NKEYS, tt), lambda i, e: (0, e, i))
    return pl.pallas_call(
        functools.partial(_peer_kernel, tcol=tcol),
        grid=(n // tt, ne // te),
        in_specs=[pl.BlockSpec((tt, d), lambda i, e: (i, 0)),
                  pl.BlockSpec((tt, d), lambda i, e: (i, 0)),
                  r1(), r1(), rt(), rt(),
                  pl.BlockSpec((te, d), lambda i, e: (e, 0)),
                  pl.BlockSpec((d, te), lambda i, e: (0, e)),
                  pl.BlockSpec((1, d), lambda i, e: (0, 0))],
        out_specs=pl.BlockSpec((tt, d), lambda i, e: (i, 0)),
        out_shape=jax.ShapeDtypeStruct((n, d), F32),
        scratch_shapes=[pltpu.VMEM((te, tt), BF16), pltpu.VMEM((te, tt), BF16),
                        pltpu.VMEM((d, tt), F32)],
        compiler_params=_params("parallel", "arbitrary"),
        name="peer_experts",
    )(xn, x1, n1, e1, r2, e2, w['expert_u'], w['expert_vt'], w['final_norm'])


def _rot_cols(wm):
    hw = wm.shape[-1] // 2
    return jnp.concatenate([-wm[..., hw:], wm[..., :hw]], axis=-1)


def _s5_direction_params(lam_re, lam_im, log_dt, b_re, b_im, c_re, c_im):
    g, p, ch = SSM_GROUPS, SSM_STATE, SSM_GROUP_CH
    dt = jnp.exp(log_dt)[:, None]
    mag = jnp.exp(lam_re * dt)
    a_re = mag * jnp.cos(lam_im * dt)
    a_im = mag * jnp.sin(lam_im * dt)
    nr = a_re - 1.0
    den = lam_re * lam_re + lam_im * lam_im
    f_re = (nr * lam_re + a_im * lam_im) / den
    f_im = (a_im * lam_re - nr * lam_im) / den
    bb_re = f_re[..., None] * b_re - f_im[..., None] * b_im
    bb_im = f_re[..., None] * b_im + f_im[..., None] * b_re
    eye = jnp.eye(g, dtype=F32)
    per_blk = SSM_CHUNKS * LANES // SSM_WIDTH

    def expand(bb):
        nb = SSM_WIDTH // LANES
        m = jnp.einsum('gph,gk->ghkp', bb, eye).reshape(nb, LANES, nb, per_blk, LANES)
        return jnp.einsum('arack->acrk', m).reshape(SSM_CHUNKS, LANES, LANES)

    def readout(cm):
        nb = SSM_WIDTH // LANES
        m = jnp.einsum('ghp,gk->gpkh', cm, eye).reshape(nb, per_blk, LANES, nb, LANES)
        return jnp.einsum('acrak->acrk', m).reshape(SSM_CHUNKS, LANES, LANES)

    wexp = jnp.concatenate([expand(bb_re), expand(bb_im)], axis=-1).astype(BF16)
    cmat = jnp.concatenate([readout(c_re), readout(-c_im)], axis=1).astype(BF16)
    a = jnp.concatenate([a_re.reshape(2, SUBLANES, LANES), a_im.reshape(2, SUBLANES, LANES)])
    return wexp, a, cmat


def _prepare(mix_norm, w_in, q_norm, kv_norm, w_uq, w_ukv, lam_re, lam_im, log_dt, b_re, b_im,
             c_re, c_im, d_skip, w_glu, b_glu, w_out, ffn_norm, w_query, sub_keys, expert_u,
             expert_v, final_norm):
    o1 = Q_LORA
    o2 = o1 + KV_LORA
    o3 = o2 + QK_ROPE
    wi = w_in[0]
    w_kpe = wi[:, o2:o3]
    w = {}
    w['mix_norm'] = mix_norm[0][None, :]
    w['w_in'] = jnp.concatenate([wi[:, :o2], w_kpe, _rot_cols(w_kpe), wi[:, o3:]], axis=1).astype(BF16)
    w['q_norm'] = q_norm[0][None, :]
    w['kv_norm'] = kv_norm[0][None, :]
    wq = w_uq[0]
    wq_pe = wq[..., QK_NOPE:]
    w['w_q'] = jnp.concatenate([wq[..., :QK_NOPE], wq_pe, _rot_cols(wq_pe)], axis=-1).reshape(
        Q_LORA, N_HEADS * QK_PAD).astype(BF16)
    w['w_kv'] = w_ukv[0].reshape(KV_LORA, N_HEADS * (QK_NOPE + V_HEAD)).astype(BF16)
    for name, d in (('f', 0), ('b', 1)):
        wexp, a, cmat = _s5_direction_params(lam_re[0, d], lam_im[0, d], log_dt[0, d], b_re[0, d],
                                             b_im[0, d], c_re[0, d], c_im[0, d])
        w['s5_w' + name], w['s5_a' + name], w['s5_c' + name] = wexp, a, cmat
    w['d_skip'] = d_skip[0].reshape(1, SSM_WIDTH)
    w['w_glu'] = w_glu[0].astype(BF16)
    w['b_glu'] = b_glu[0][None, :]
    mla_w = N_HEADS * V_HEAD
    w['w_out_att'] = w_out[0][:mla_w].astype(BF16)
    w['w_out_ssm'] = w_out[0][mla_w:].astype(BF16)
    w['ffn_norm'] = ffn_norm[0][None, :]
    w['w_query'] = w_query[0].astype(BF16)
    w['sub_keys'] = sub_keys[0].astype(BF16)
    w['expert_u'] = expert_u[0].astype(BF16)
    w['expert_vt'] = expert_v[0].astype(BF16).T
    w['final_norm'] = final_norm[None, :]
    return w


def _rope_table(s):
    inv = ROPE_THETA ** (-jnp.arange(0, QK_ROPE, 2, dtype=F32) / QK_ROPE)
    ang = jnp.arange(s, dtype=F32)[:, None] * inv[None, :]
    c, sn = jnp.cos(ang), jnp.sin(ang)
    return jnp.concatenate([c, c, sn, sn], axis=1)


def _encoder(x, w):
    bsz, s, d = x.shape
    q, k, v, u = _in_stage(x, _rope_table(s), w)
    att = _attention(q, k, v)
    yf, yb = _s5_scan(u, w)
    n = bsz * s
    x1, xn, qp = _mid_stage(x.reshape(n, d), att.reshape(n, -1), yf.reshape(n, -1),
                            yb.reshape(n, -1), u.reshape(n, -1), w)
    n1, e1, r2, e2 = _route(qp, w['sub_keys'])
    y = _peer(xn, x1, n1, e1, r2, e2, w)
    return y.reshape(bsz, s, d)


def kernel(x_prompt, x_sample, mix_norm, w_in, q_norm, kv_norm, w_uq, w_ukv, lam_re, lam_im, log_dt, b_re, b_im, c_re, c_im, d_skip, w_glu, b_glu, w_out, ffn_norm, w_query, sub_keys, expert_u, expert_v, final_norm):
    w = _prepare(mix_norm, w_in, q_norm, kv_norm, w_uq, w_ukv, lam_re, lam_im, log_dt, b_re, b_im,
                 c_re, c_im, d_skip, w_glu, b_glu, w_out, ffn_norm, w_query, sub_keys, expert_u,
                 expert_v, final_norm)
    return (_encoder(x_prompt, w), _encoder(x_sample, w))
```
